```python
import jax, jax.numpy as jnp
from jax import lax
import numpy as np

D_MODEL = 1024
BATCH = 16
SEQ = 4096
DEPTH = 1

MIX_WIDTH = D_MODEL
M_HEADS = 4
M_WIDTH = MIX_WIDTH // 2
M_HEAD_DIM = M_WIDTH // M_HEADS
CHUNK = 64
R_HEADS = 4
R_WIDTH = MIX_WIDTH - M_WIDTH
R_BLOCK = R_WIDTH // R_HEADS
CONV_WIDTH = 4
LRU_C = 8.0
N_EXPERTS = 32
TOP_K = 4
D_FF = D_MODEL
SWIGLU_LIMIT = 7.0
SWIGLU_ALPHA = 1.702
MOE_BLOCK = 512
EPS = 1e-6

Q_OFF = 0
K_OFF = Q_OFF + M_WIDTH
V_OFF = K_OFF + M_WIDTH
G_OFF = V_OFF + M_WIDTH
O_OFF = G_OFF + 2 * M_HEADS
RX_OFF = O_OFF + M_WIDTH
RG_OFF = RX_OFF + R_WIDTH
IN_COLS = RG_OFF + R_WIDTH

kernel_name = "hymba_mlstm_rglru_moe_block"


def rms_norm(x, w):
    xf = x.astype(jnp.float32)
    y = xf * lax.rsqrt(jnp.mean(xf * xf, axis=-1, keepdims=True) + EPS)
    return (y * w.astype(jnp.float32)).astype(x.dtype)


def mlstm_chunkwise(q, k, v, ig, fg):
    B, S, H, d = q.shape
    nc = S // CHUNK
    q = q.astype(jnp.float32)
    k = k.astype(jnp.float32) * (d ** -0.5)
    v = v.astype(jnp.float32)
    ig = ig.astype(jnp.float32)
    lf = jax.nn.log_sigmoid(fg.astype(jnp.float32))

    def chunks4(t):
        return t.reshape(B, nc, CHUNK, H, d).transpose(1, 0, 3, 2, 4)

    def chunks3(t):
        return t.reshape(B, nc, CHUNK, H).transpose(1, 0, 3, 2)

    causal = jnp.tril(jnp.ones((CHUNK, CHUNK), dtype=bool))

    def body(carry, xs):
        C, n, m = carry
        qc, kc, vc, ic, fc = xs
        b = jnp.cumsum(fc, axis=-1)
        Dm = b[..., :, None] - b[..., None, :] + ic[..., None, :]
        Dm = jnp.where(causal, Dm, -jnp.inf)
        inter = b + m[..., None]
        m_t = jnp.maximum(jnp.max(Dm, axis=-1), inter)
        Sw = jnp.einsum('bhtd,bhsd->bhts', qc, kc) * jnp.exp(Dm - m_t[..., None])
        w_inter = jnp.exp(inter - m_t)
        num = jnp.einsum('bhts,bhse->bhte', Sw, vc) + w_inter[..., None] * jnp.einsum('bhtd,bhde->bhte', qc, C)
        den = jnp.sum(Sw, axis=-1) + w_inter * jnp.einsum('bhtd,bhd->bht', qc, n)
        h = num / jnp.maximum(jnp.abs(den), jnp.exp(-m_t))[..., None]
        bL = b[..., -1]
        g = bL[..., None] - b + ic
        m_new = jnp.maximum(bL + m, jnp.max(g, axis=-1))
        w = jnp.exp(g - m_new[..., None])
        decay = jnp.exp(bL + m - m_new)
        C_new = decay[..., None, None] * C + jnp.einsum('bhs,bhsd,bhse->bhde', w, kc, vc)
        n_new = decay[..., None] * n + jnp.einsum('bhs,bhsd->bhd', w, kc)
        return (C_new, n_new, m_new), h

    init = (jnp.zeros((B, H, d, d), jnp.float32), jnp.zeros((B, H, d), jnp.float32),
            jnp.zeros((B, H), jnp.float32))
    _, hs = lax.scan(body, init, (chunks4(q), chunks4(k), chunks4(v), chunks3(ig), chunks3(lf)))
    return hs.transpose(1, 0, 3, 2, 4).reshape(B, S, H, d)


def causal_depthwise_conv(x, w, b):
    S = x.shape[1]
    xp = jnp.pad(x, ((0, 0), (CONV_WIDTH - 1, 0), (0, 0)))
    y = sum(xp[:, j:j + S] * w[j] for j in range(CONV_WIDTH))
    return y + b


def rg_lru(x, wr, br, wi, bi, lam):
    B, S, R = x.shape
    xb = x.reshape(B, S, R_HEADS, R_BLOCK)
    r = jax.nn.sigmoid(jnp.einsum('bshi,hij->bshj', xb, wr.astype(jnp.float32)).reshape(B, S, R) + br)
    i = jax.nn.sigmoid(jnp.einsum('bshi,hij->bshj', xb, wi.astype(jnp.float32)).reshape(B, S, R) + bi)
    log_a = -LRU_C * r * jax.nn.softplus(-lam.astype(jnp.float32))
    a = jnp.exp(log_a)
    bterm = jnp.sqrt(-jnp.expm1(2.0 * log_a)) * (i * x)

    def combine(p, q):
        a1, b1 = p
        a2, b2 = q
        return a1 * a2, a2 * b1 + b2

    _, h = lax.associative_scan(combine, (a, bterm), axis=1)
    return h


def moe_ffn(x, router_w, router_b, w_gu, b_gu, w_down, b_down):
    B, S, D = x.shape
    T = B * S
    xt = x.reshape(T, D)
    logits = (xt @ router_w).astype(jnp.float32) + router_b.astype(jnp.float32)
    topv, topi = lax.top_k(logits, TOP_K)
    gates = jax.nn.softmax(topv, axis=-1)
    A = T * TOP_K
    e = topi.reshape(A)
    tok = jnp.arange(A, dtype=jnp.int32) // TOP_K
    g = gates.reshape(A)
    order = jnp.argsort(e)
    e_s, tok_s, g_s = e[order], tok[order], g[order]
    counts = jnp.zeros((N_EXPERTS,), jnp.int32).at[e].add(1)
    starts = jnp.cumsum(counts) - counts
    padded = ((counts + MOE_BLOCK - 1) // MOE_BLOCK) * MOE_BLOCK
    pends = jnp.cumsum(padded)
    pstarts = pends - padded
    dest = pstarts[e_s] + jnp.arange(A, dtype=jnp.int32) - starts[e_s]
    NB = -(-A // MOE_BLOCK) + N_EXPERTS
    P = NB * MOE_BLOCK
    src = jnp.zeros((P,), jnp.int32).at[dest].set(tok_s)
    wrow = jnp.zeros((P,), jnp.float32).at[dest].set(g_s)
    block_e = jnp.clip(jnp.searchsorted(pends, jnp.arange(NB, dtype=jnp.int32) * MOE_BLOCK, side='right'),
                       0, N_EXPERTS - 1)

    def body(acc, blk):
        be, bsrc, bw = blk
        xb = xt[bsrc]
        gu = xb @ w_gu[be] + b_gu[be]
        gate = jnp.minimum(gu[:, :D_FF], SWIGLU_LIMIT)
        up = jnp.clip(gu[:, D_FF:], -SWIGLU_LIMIT, SWIGLU_LIMIT)
        hdn = (up + 1.0) * gate * jax.nn.sigmoid(gate * SWIGLU_ALPHA)
        yb = hdn @ w_down[be] + b_down[be]
        acc = acc.at[bsrc].add((yb * bw[:, None]).astype(acc.dtype))
        return acc, None

    out, _ = lax.scan(body, jnp.zeros((T, D), x.dtype),
                      (block_e, src.reshape(NB, MOE_BLOCK), wrow.reshape(NB, MOE_BLOCK)))
    return out.reshape(B, S, D)


def setup_inputs(seed: int = 0) -> dict:
    key = jax.random.key(seed)
    ks = jax.random.split(key, 24)
    f32 = jnp.float32
    L, D, E = DEPTH, D_MODEL, N_EXPERTS

    def nrm(k, shape, scale):
        return jax.random.normal(k, shape, f32) * scale

    x = jax.random.normal(ks[0], (BATCH, SEQ, D), f32)
    norm1_w = 1.0 + nrm(ks[1], (L, D), 0.01)
    w_in = nrm(ks[2], (L, D, IN_COLS), D ** -0.5)
    i_bias = nrm(ks[3], (L, M_HEADS), 0.1)
    f_bias = jnp.linspace(3.0, 6.0, M_HEADS, dtype=f32)[None, :] + nrm(ks[4], (L, M_HEADS), 0.1)
    mlstm_gate_b = jnp.concatenate([i_bias, f_bias], axis=-1)
    mlstm_norm_w = 1.0 + nrm(ks[5], (L, M_WIDTH), 0.01)
    conv_w = nrm(ks[6], (L, CONV_WIDTH, R_WIDTH), CONV_WIDTH ** -0.5)
    conv_b = nrm(ks[7], (L, R_WIDTH), 0.01)
    lru_wr = nrm(ks[8], (L, R_HEADS, R_BLOCK, R_BLOCK), R_BLOCK ** -0.5)
    lru_br = nrm(ks[9], (L, R_WIDTH), 0.01)
    lru_wi = nrm(ks[10], (L, R_HEADS, R_BLOCK, R_BLOCK), R_BLOCK ** -0.5)
    lru_bi = nrm(ks[11], (L, R_WIDTH), 0.01)
    u = jax.random.uniform(ks[12], (L, R_WIDTH), f32, 0.9, 0.999)
    p = u ** (1.0 / LRU_C)
    lru_lambda = jnp.log(p) - jnp.log1p(-p)
    lru_norm_w = 1.0 + nrm(ks[13], (L, R_WIDTH), 0.01)
    w_out = nrm(ks[14], (L, MIX_WIDTH, D), MIX_WIDTH ** -0.5)
    norm2_w = 1.0 + nrm(ks[15], (L, D), 0.01)
    router_w = nrm(ks[16], (L, D, E), D ** -0.5)
    router_b = nrm(ks[17], (L, E), 0.01)
    moe_w_gu = nrm(ks[18], (L, E, D, 2 * D_FF), D ** -0.5)
    moe_b_gu = nrm(ks[19], (L, E, 2 * D_FF), 0.01)
    moe_w_down = nrm(ks[20], (L, E, D_FF, D), D_FF ** -0.5)
    moe_b_down = nrm(ks[21], (L, E, D), 0.01)
    final_norm_w = 1.0 + nrm(ks[22], (D,), 0.01)
    return {"x": x, "norm1_w": norm1_w, "w_in": w_in, "mlstm_gate_b": mlstm_gate_b,
            "mlstm_norm_w": mlstm_norm_w, "conv_w": conv_w, "conv_b": conv_b,
            "lru_wr": lru_wr, "lru_br": lru_br, "lru_wi": lru_wi, "lru_bi": lru_bi,
            "lru_lambda": lru_lambda, "lru_norm_w": lru_norm_w, "w_out": w_out,
            "norm2_w": norm2_w, "router_w": router_w, "router_b": router_b,
            "moe_w_gu": moe_w_gu, "moe_b_gu": moe_b_gu, "moe_w_down": moe_w_down,
            "moe_b_down": moe_b_down, "final_norm_w": final_norm_w}


def reference(x, norm1_w, w_in, mlstm_gate_b, mlstm_norm_w, conv_w, conv_b, lru_wr, lru_br, lru_wi,
              lru_bi, lru_lambda, lru_norm_w, w_out, norm2_w, router_w, router_b, moe_w_gu, moe_b_gu,
              moe_w_down, moe_b_down, final_norm_w):
    B, S, D = x.shape
    h = x
    for l in range(DEPTH):
        hn = rms_norm(h, norm1_w[l])
        proj = hn @ w_in[l]
        q = proj[..., Q_OFF:K_OFF].reshape(B, S, M_HEADS, M_HEAD_DIM)
        k = proj[..., K_OFF:V_OFF].reshape(B, S, M_HEADS, M_HEAD_DIM)
        v = proj[..., V_OFF:G_OFF].reshape(B, S, M_HEADS, M_HEAD_DIM)
        gts = proj[..., G_OFF:O_OFF].astype(jnp.float32) + mlstm_gate_b[l].astype(jnp.float32)
        h_m = mlstm_chunkwise(q, k, v, gts[..., :M_HEADS], gts[..., M_HEADS:])
        o = jax.nn.sigmoid(proj[..., O_OFF:RX_OFF].astype(jnp.float32)).reshape(B, S, M_HEADS, M_HEAD_DIM)
        h_m = rms_norm(o * h_m, mlstm_norm_w[l].reshape(M_HEADS, M_HEAD_DIM)).reshape(B, S, M_WIDTH)
        xr = proj[..., RX_OFF:RG_OFF].astype(jnp.float32)
        xr = causal_depthwise_conv(xr, conv_w[l].astype(jnp.float32), conv_b[l].astype(jnp.float32))
        h_r = rg_lru(xr, lru_wr[l], lru_br[l].astype(jnp.float32), lru_wi[l],
                     lru_bi[l].astype(jnp.float32), lru_lambda[l])
        gate_r = jax.nn.gelu(proj[..., RG_OFF:IN_COLS].astype(jnp.float32), approximate=True)
        h_r = rms_norm(gate_r * h_r, lru_norm_w[l])
        mix = jnp.concatenate([h_m, h_r], axis=-1).astype(h.dtype) @ w_out[l]
        h = h + mix
        h = h + moe_ffn(rms_norm(h, norm2_w[l]), router_w[l], router_b[l], moe_w_gu[l], moe_b_gu[l],
                        moe_w_down[l], moe_b_down[l])
    return rms_norm(h, final_norm_w)
```

```python
import functools

import jax
import jax.numpy as jnp
from jax import lax
from jax.experimental import pallas as pl
from jax.experimental.pallas import tpu as pltpu

F32 = jnp.float32
BF16 = jnp.bfloat16
I32 = jnp.int32

D_MODEL = 1024
M_HEADS = 4
HEAD_DIM = 128
M_WIDTH = M_HEADS * HEAD_DIM
R_HEADS = 4
R_BLOCK = 128
R_WIDTH = R_HEADS * R_BLOCK
CONV_WIDTH = 4
LRU_C = 8.0
N_EXPERTS = 32
TOP_K = 4
D_FF = 1024
SWIGLU_LIMIT = 7.0
SWIGLU_ALPHA = 1.702
EPS = 1e-6

N_MAIN = 6 * 512
LANES = 128
SUBLANES = 8
VMEM_LIMIT = 56 * 1024 * 1024

TM_PROJ = 512
TS_SEQ = 512
CHUNK = 128
TN_ROUTE = 2048
TT_ROWS = 256
TM_MOE = 512
FF_CHUNK = 512
NEG_BIG = -1e30


def _cparams(sem, vmem=VMEM_LIMIT):
    return pltpu.CompilerParams(dimension_semantics=sem, vmem_limit_bytes=vmem)


def _rms(x, w):
    ms = jnp.mean(x * x, axis=-1, keepdims=True)
    return x * lax.rsqrt(ms + EPS) * w


def _inproj_kernel(x_ref, nw_ref, w_ref, wg_ref, gb_ref, proj_ref, ig_ref, fg_ref):
    xn = _rms(x_ref[...], nw_ref[...]).astype(BF16)
    for c in range(N_MAIN // 512):
        cols = slice(c * 512, (c + 1) * 512)
        proj_ref[:, cols] = jnp.dot(xn, w_ref[:, cols], preferred_element_type=F32).astype(BF16)
    g = jnp.dot(xn, wg_ref[...], preferred_element_type=F32)
    gt = g.T + gb_ref[...]
    ig_ref[...] = gt[0:8]
    fg_ref[...] = gt[8:16]


def _inproj(x2, nw, w_main, w_gate, gate_b):
    T = x2.shape[0]
    tm = TM_PROJ
    return pl.pallas_call(
        _inproj_kernel,
        grid=(T // tm,),
        in_specs=[
            pl.BlockSpec((tm, D_MODEL), lambda i: (i, 0)),
            pl.BlockSpec((1, D_MODEL), lambda i: (0, 0)),
            pl.BlockSpec((D_MODEL, N_MAIN), lambda i: (0, 0)),
            pl.BlockSpec((D_MODEL, LANES), lambda i: (0, 0)),
            pl.BlockSpec((LANES, 1), lambda i: (0, 0)),
        ],
        out_specs=[
            pl.BlockSpec((tm, N_MAIN), lambda i: (i, 0)),
            pl.BlockSpec((SUBLANES, tm), lambda i: (0, i)),
            pl.BlockSpec((SUBLANES, tm), lambda i: (0, i)),
        ],
        out_shape=[
            jax.ShapeDtypeStruct((T, N_MAIN), BF16),
            jax.ShapeDtypeStruct((SUBLANES, T), F32),
            jax.ShapeDtypeStruct((SUBLANES, T), F32),
        ],
        compiler_params=_cparams(("arbitrary",)),
        name="inproj",
    )(x2, nw, w_main, w_gate, gate_b)


def _segmented_scan(x, op, ident, lane_in_chunk):
    d = 1
    while d < CHUNK:
        shifted = pltpu.roll(x, d, axis=1)
        x = op(x, jnp.where(lane_in_chunk >= d, shifted, ident))
        d *= 2
    return x


def _mlstm_kernel(q_ref, k_ref, v_ref, o_ref, ig_ref, fg_ref, nw_ref, out_ref,
                  c_ref, m_ref, rt_ref):
    ts = q_ref.shape[0]
    scale = HEAD_DIM ** -0.5

    @pl.when(pl.program_id(1) == 0)
    def _():
        c_ref[...] = jnp.zeros_like(c_ref)
        m_ref[...] = jnp.zeros_like(m_ref)

    ig = ig_ref[...]
    fg = fg_ref[...]
    lf = jnp.minimum(fg, 0.0) - jnp.log1p(jnp.exp(-jnp.abs(fg)))
    lane_in_chunk = lax.broadcasted_iota(I32, (SUBLANES, ts), 1) % CHUNK
    b_row = _segmented_scan(lf, jnp.add, 0.0, lane_in_chunk)
    g_row = ig - b_row
    cm_row = _segmented_scan(g_row, jnp.maximum, NEG_BIG, lane_in_chunk)
    stacked = jnp.concatenate(
        [b_row, g_row, cm_row, jnp.zeros((LANES - 3 * SUBLANES, ts), F32)], axis=0)
    rt_ref[...] = stacked.T

    t_idx = lax.broadcasted_iota(I32, (CHUNK, CHUNK), 0)
    s_idx = lax.broadcasted_iota(I32, (CHUNK, CHUNK), 1)
    causal = t_idx >= s_idx
    ones_col = (lax.broadcasted_iota(I32, (CHUNK, HEAD_DIM), 1) == 0).astype(BF16)

    for c in range(ts // CHUNK):
        rows = slice(c * CHUNK, (c + 1) * CHUNK)
        last = (c + 1) * CHUNK - 1
        for h in range(M_HEADS):
            cols = slice(h * HEAD_DIM, (h + 1) * HEAD_DIM)
            qc = q_ref[rows, cols]
            kc = k_ref[rows, cols]
            v_ext = jnp.concatenate([v_ref[rows, cols], ones_col], axis=1)
            b_col = rt_ref[rows, h:h + 1]
            g_col = rt_ref[rows, SUBLANES + h:SUBLANES + h + 1]
            cm_col = rt_ref[rows, 2 * SUBLANES + h:2 * SUBLANES + h + 1]
            g_r = g_row[h:h + 1, rows]
            b_last = b_row[h:h + 1, last:last + 1]
            cm_last = cm_row[h:h + 1, last:last + 1]
            m_prev = m_ref[h:h + 1, 0:1]
            c_prev = c_ref[h]

            mx = jnp.maximum(cm_col, m_prev)
            decay_mat = jnp.exp(jnp.where(causal, g_r - mx, NEG_BIG))
            scores = lax.dot_general(qc, kc, (((1,), (1,)), ((), ())),
                                     preferred_element_type=F32)
            p = (scores * decay_mat * scale).astype(BF16)
            intra = jnp.dot(p, v_ext, preferred_element_type=F32)
            inter = jnp.dot(qc, c_prev.astype(BF16), preferred_element_type=F32)
            w_inter = jnp.exp(m_prev - mx)
            tot = intra + w_inter * inter
            num = tot[:, :HEAD_DIM]
            den = tot[:, HEAD_DIM:HEAD_DIM + 1]
            m_t = b_col + mx
            h_t = num / jnp.maximum(jnp.abs(den), jnp.exp(-m_t))

            mx_last = jnp.maximum(cm_last, m_prev)
            w_col = jnp.exp(g_col - mx_last) * scale
            kw_t = (kc.astype(F32) * w_col).T.astype(BF16)
            upd = jnp.dot(kw_t, v_ext, preferred_element_type=F32)
            c_ref[h] = jnp.exp(m_prev - mx_last) * c_prev + upd
            m_ref[h:h + 1, :] = jnp.broadcast_to(b_last + mx_last, (1, LANES))

            og = jax.nn.sigmoid(o_ref[rows, cols].astype(F32))
            out_ref[rows, cols] = _rms(og * h_t, nw_ref[:, cols]).astype(BF16)


def _mlstm(proj, ig, fg, nw, B, S):
    T = B * S
    ts = TS_SEQ
    nst = S // ts

    def col(j):
        return pl.BlockSpec((ts, M_WIDTH), lambda b, s, j=j: (b * nst + s, j))

    gate_spec = pl.BlockSpec((SUBLANES, ts), lambda b, s: (0, b * nst + s))
    return pl.pallas_call(
        _mlstm_kernel,
        grid=(B, nst),
        in_specs=[col(0), col(1), col(2), col(3), gate_spec, gate_spec,
                  pl.BlockSpec((1, M_WIDTH), lambda b, s: (0, 0))],
        out_specs=pl.BlockSpec((ts, M_WIDTH), lambda b, s: (b * nst + s, 0)),
        out_shape=jax.ShapeDtypeStruct((T, M_WIDTH), BF16),
        scratch_shapes=[
            pltpu.VMEM((M_HEADS, HEAD_DIM, 2 * HEAD_DIM), F32),
            pltpu.VMEM((SUBLANES, LANES), F32),
            pltpu.VMEM((ts, LANES), F32),
        ],
        compiler_params=_cparams(("arbitrary", "arbitrary")),
        name="mlstm",
    )(proj, proj, proj, proj, ig, fg, nw)


def _gelu_tanh(x):
    return 0.5 * x * (1.0 + jnp.tanh(0.7978845608028654 * (x + 0.044715 * (x * x * x))))


def _rglru_kernel(rx_ref, rg_ref, cw_ref, cb_ref, wri_ref, bri_ref, lam_ref, nw_ref, out_ref,
                  xpad_ref, a_ref, b_ref, hcar_ref):
    ts = rx_ref.shape[0]

    @pl.when(pl.program_id(1) == 0)
    def _():
        xpad_ref[0:SUBLANES, :] = jnp.zeros((SUBLANES, R_WIDTH), F32)
        hcar_ref[...] = jnp.zeros_like(hcar_ref)

    x = rx_ref[...].astype(F32)
    xpad_ref[SUBLANES:SUBLANES + ts, :] = x
    xp = xpad_ref[...]
    xc = cb_ref[...]
    for j in range(CONV_WIDTH):
        sh = CONV_WIDTH - 1 - j
        tap = xp if sh == 0 else pltpu.roll(xp, sh, axis=0)
        xc = xc + cw_ref[j:j + 1, :] * tap[SUBLANES:SUBLANES + ts, :]
    xpad_ref[0:SUBLANES, :] = x[ts - SUBLANES:ts, :]

    lam = lam_ref[...]
    nlam = -lam
    softplus_nlam = jnp.maximum(nlam, 0.0) + jnp.log1p(jnp.exp(-jnp.abs(nlam)))
    row_in_group = lax.broadcasted_iota(I32, (ts, R_BLOCK), 0) % SUBLANES

    for h in range(R_HEADS):
        cols = slice(h * R_BLOCK, (h + 1) * R_BLOCK)
        xh = xc[:, cols]
        ri = jnp.dot(xh.astype(BF16), wri_ref[h], preferred_element_type=F32) + bri_ref[h]
        r = jax.nn.sigmoid(ri[:, :R_BLOCK])
        i = jax.nn.sigmoid(ri[:, R_BLOCK:])
        log_a = -LRU_C * r * softplus_nlam[:, cols]
        a = jnp.exp(log_a)
        one_m_a2 = -jnp.tanh(log_a) * (a * a + 1.0)
        bt = jnp.sqrt(one_m_a2) * (i * xh)
        d = 1
        while d < SUBLANES:
            keep = row_in_group >= d
            a_sh = pltpu.roll(a, d, axis=0)
            b_sh = pltpu.roll(bt, d, axis=0)
            bt = bt + jnp.where(keep, a * b_sh, 0.0)
            a = a * jnp.where(keep, a_sh, 1.0)
            d *= 2
        a_ref[:, cols] = a
        b_ref[:, cols] = bt

    def group_step(g, hprev):
        r0 = pl.multiple_of(g * SUBLANES, SUBLANES)
        hr = b_ref[pl.ds(r0, SUBLANES), :] + a_ref[pl.ds(r0, SUBLANES), :] * hprev
        b_ref[pl.ds(r0, SUBLANES), :] = hr
        return hr[SUBLANES - 1:SUBLANES, :]

    hcar_ref[...] = lax.fori_loop(0, ts // SUBLANES, group_step, hcar_ref[...], unroll=4)

    y = _gelu_tanh(rg_ref[...].astype(F32)) * b_ref[...]
    out_ref[...] = _rms(y, nw_ref[...]).astype(BF16)


def _rglru(proj, cw, cb, wri, bri, lam, nw, B, S):
    T = B * S
    ts = TS_SEQ
    nst = S // ts

    def col(j):
        return pl.BlockSpec((ts, R_WIDTH), lambda b, s, j=j: (b * nst + s, j))

    def const(shape):
        return pl.BlockSpec(shape, lambda b, s: (0,) * len(shape))

    return pl.pallas_call(
        _rglru_kernel,
        grid=(B, nst),
        in_specs=[col(4), col(5), const((CONV_WIDTH, R_WIDTH)), const((1, R_WIDTH)),
                  const((R_HEADS, R_BLOCK, 2 * R_BLOCK)), const((R_HEADS, 1, 2 * R_BLOCK)),
                  const((1, R_WIDTH)), const((1, R_WIDTH))],
        out_specs=pl.BlockSpec((ts, R_WIDTH), lambda b, s: (b * nst + s, 0)),
        out_shape=jax.ShapeDtypeStruct((T, R_WIDTH), BF16),
        scratch_shapes=[
            pltpu.VMEM((ts + SUBLANES, R_WIDTH), F32),
            pltpu.VMEM((ts, R_WIDTH), F32),
            pltpu.VMEM((ts, R_WIDTH), F32),
            pltpu.VMEM((1, R_WIDTH), F32),
        ],
        compiler_params=_cparams(("arbitrary", "arbitrary")),
        name="rglru",
    )(proj, proj, cw, cb, wri, bri, lam, nw)


def _outproj_kernel(hm_ref, hr_ref, x_ref, wo_ref, n2_ref, rwh_ref, rwl_ref, rb_ref,
                    h1_ref, xn_ref, lg_ref):
    mix = jnp.dot(hm_ref[...], wo_ref[0:M_WIDTH, :], preferred_element_type=F32)
    mix = mix + jnp.dot(hr_ref[...], wo_ref[M_WIDTH:, :], preferred_element_type=F32)
    h1 = x_ref[...] + mix
    h1_ref[...] = h1
    xn = _rms(h1, n2_ref[...])
    xn_ref[...] = xn
    xh = xn.astype(BF16)
    xl = (xn - xh.astype(F32)).astype(BF16)
    lg = jnp.dot(xh, rwh_ref[...], preferred_element_type=F32)
    lg = lg + jnp.dot(xl, rwh_ref[...], preferred_element_type=F32)
    lg = lg + jnp.dot(xh, rwl_ref[...], preferred_element_type=F32)
    lg_ref[...] = lg.T[0:N_EXPERTS] + rb_ref[...]


def _outproj(hm, hr, x2, wo, n2, rwh, rwl, rb):
    T = x2.shape[0]
    tm = TM_PROJ

    def const(shape):
        return pl.BlockSpec(shape, lambda i: (0,) * len(shape))

    return pl.pallas_call(
        _outproj_kernel,
        grid=(T // tm,),
        in_specs=[
            pl.BlockSpec((tm, M_WIDTH), lambda i: (i, 0)),
            pl.BlockSpec((tm, R_WIDTH), lambda i: (i, 0)),
            pl.BlockSpec((tm, D_MODEL), lambda i: (i, 0)),
            const((D_MODEL, D_MODEL)), const((1, D_MODEL)),
            const((D_MODEL, LANES)), const((D_MODEL, LANES)), const((N_EXPERTS, 1)),
        ],
        out_specs=[
            pl.BlockSpec((tm, D_MODEL), lambda i: (i, 0)),
            pl.BlockSpec((tm, D_MODEL), lambda i: (i, 0)),
            pl.BlockSpec((N_EXPERTS, tm), lambda i: (0, i)),
        ],
        out_shape=[
            jax.ShapeDtypeStruct((T, D_MODEL), F32),
            jax.ShapeDtypeStruct((T, D_MODEL), F32),
            jax.ShapeDtypeStruct((N_EXPERTS, T), F32),
        ],
        compiler_params=_cparams(("arbitrary",)),
        name="outproj",
    )(hm, hr, x2, wo, n2, rwh, rwl, rb)


def _route_kernel(lg_ref, ids_ref, gates_ref, rank_ref, cnt_ref, carry_ref):
    tn = lg_ref.shape[1]
    sub = 256

    @pl.when(pl.program_id(0) == 0)
    def _():
        carry_ref[...] = jnp.zeros_like(carry_ref)

    logits = lg_ref[...]
    e_idx = lax.broadcasted_iota(I32, (N_EXPERTS, tn), 0)
    vals, idxs, sels = [], [], []
    for _ in range(TOP_K):
        m = jnp.max(logits, axis=0, keepdims=True)
        idx = jnp.min(jnp.where(logits == m, e_idx, N_EXPERTS), axis=0, keepdims=True)
        sel = e_idx == idx
        logits = jnp.where(sel, -jnp.inf, logits)
        vals.append(m)
        idxs.append(idx)
        sels.append(sel)
    exps = [jnp.exp(v - vals[0]) for v in vals]
    denom = exps[0] + exps[1] + exps[2] + exps[3]
    chosen = (sels[0] | sels[1] | sels[2] | sels[3]).astype(BF16)

    s_i = lax.broadcasted_iota(I32, (sub, sub), 0)
    t_i = lax.broadcasted_iota(I32, (sub, sub), 1)
    upper = (s_i < t_i).astype(BF16)
    carry = carry_ref[:, 0:1]
    parts = []
    for c in range(tn // sub):
        mc = chosen[:, c * sub:(c + 1) * sub]
        cs = jnp.dot(mc, upper, preferred_element_type=F32) + carry
        parts.append(cs)
        carry = cs[:, sub - 1:sub] + mc[:, sub - 1:sub].astype(F32)
    before = jnp.concatenate(parts, axis=1)
    carry_ref[...] = jnp.broadcast_to(carry, carry_ref.shape)
    cnt_ref[...] = jnp.broadcast_to(carry, cnt_ref.shape)

    zeros_i = jnp.zeros((SUBLANES - TOP_K, tn), I32)
    zeros_f = jnp.zeros((SUBLANES - TOP_K, tn), F32)
    ranks = [jnp.sum(jnp.where(s, before, 0.0), axis=0, keepdims=True).astype(I32) for s in sels]
    ids_ref[...] = jnp.concatenate(idxs + [zeros_i], axis=0)
    gates_ref[...] = jnp.concatenate([e / denom for e in exps] + [zeros_f], axis=0)
    rank_ref[...] = jnp.concatenate(ranks + [zeros_i], axis=0)


def _route(logits_t):
    T = logits_t.shape[1]
    tn = TN_ROUTE
    tok = pl.BlockSpec((SUBLANES, tn), lambda i: (0, i))
    return pl.pallas_call(
        _route_kernel,
        grid=(T // tn,),
        in_specs=[pl.BlockSpec((N_EXPERTS, tn), lambda i: (0, i))],
        out_specs=[tok, tok, tok, pl.BlockSpec((N_EXPERTS, LANES), lambda i: (0, 0))],
        out_shape=[
            jax.ShapeDtypeStruct((SUBLANES, T), I32),
            jax.ShapeDtypeStruct((SUBLANES, T), F32),
            jax.ShapeDtypeStruct((SUBLANES, T), I32),
            jax.ShapeDtypeStruct((N_EXPERTS, LANES), F32),
        ],
        scratch_shapes=[pltpu.VMEM((N_EXPERTS, LANES), F32)],
        compiler_params=_cparams(("arbitrary",)),
        name="route",
    )(logits_t)


def _dest_kernel(pst_ref, ids_ref, rank_ref, dest_ref):
    ids = ids_ref[...]
    start = jnp.zeros(ids.shape, I32)
    for e in range(N_EXPERTS):
        start = jnp.where(ids == e, pst_ref[e], start)
    dest_ref[0] = rank_ref[...] + start


def _dest(pstarts, ids, rank):
    T = ids.shape[1]
    tt = TT_ROWS
    tok = pl.BlockSpec((SUBLANES, tt), lambda i, pst: (0, i))
    return pl.pallas_call(
        _dest_kernel,
        grid_spec=pltpu.PrefetchScalarGridSpec(
            num_scalar_prefetch=1,
            grid=(T // tt,),
            in_specs=[tok, tok],
            out_specs=pl.BlockSpec((1, SUBLANES, tt), lambda i, pst: (i, 0, 0)),
        ),
        out_shape=jax.ShapeDtypeStruct((T // tt, SUBLANES, tt), I32),
        compiler_params=_cparams(("arbitrary",)),
        name="dest",
    )(pstarts, ids, rank)


def _dispatch_kernel(cnt_ref, pst_ref, pad_ref, nb_ref, dest_hbm, x_ref, xs_hbm,
                     dsm, zbuf, sem_d, sem_x, sem_z):
    tt = x_ref.shape[0]
    i = pl.program_id(0)
    n = pl.num_programs(0)
    slot = i % 2
    n_blocks = xs_hbm.shape[0] // TM_MOE

    def dest_copy(j, sl):
        return pltpu.make_async_copy(dest_hbm.at[j], dsm.at[sl], sem_d.at[sl])

    def zero_copy(r):
        return pltpu.make_async_copy(zbuf.at[pl.ds(0, 1), :], xs_hbm.at[pl.ds(r, 1), :], sem_z)

    def zero_block_copy(b):
        r0 = pl.multiple_of(b * TM_MOE, TM_MOE)
        return pltpu.make_async_copy(zbuf, xs_hbm.at[pl.ds(r0, TM_MOE), :], sem_z)

    def for_each_pad_row(fn):
        def per_expert(e, carry):
            lo = pst_ref[e] + cnt_ref[e]
            hi = pst_ref[e] + pad_ref[e]

            def per_row(r, c2):
                fn(r)
                return c2

            return lax.fori_loop(lo, hi, per_row, carry)

        lax.fori_loop(0, N_EXPERTS, per_expert, 0)

    def for_each_unused_block(fn):
        def per_block(b, carry):
            fn(b)
            return carry

        lax.fori_loop(nb_ref[0], n_blocks, per_block, 0)

    @pl.when(i == 0)
    def _():
        dest_copy(0, 0).start()
        zbuf[...] = jnp.zeros_like(zbuf)
        for_each_pad_row(lambda r: zero_copy(r).start())
        for_each_unused_block(lambda b: zero_block_copy(b).start())

    dest_copy(i, slot).wait()

    @pl.when(i + 1 < n)
    def _():
        dest_copy(i + 1, 1 - slot).start()

    def per_token(t, carry):
        for k in range(TOP_K):
            d = dsm[slot, k, t]
            pltpu.make_async_copy(x_ref.at[pl.ds(t, 1), :], xs_hbm.at[pl.ds(d, 1), :],
                                  sem_x).start()
        return carry

    lax.fori_loop(0, tt, per_token, 0, unroll=8)

    @pl.when(i == 0)
    def _():
        for_each_pad_row(lambda r: zero_copy(r).wait())
        for_each_unused_block(lambda b: zero_block_copy(b).wait())

    for k in range(TOP_K):
        pltpu.make_async_copy(x_ref, xs_hbm.at[pl.ds(0, tt), :], sem_x).wait()


def _dispatch(counts, pstarts, padded, nb_used, dest_tiles, xn, n_rows):
    T = xn.shape[0]
    tt = TT_ROWS
    return pl.pallas_call(
        _dispatch_kernel,
        grid_spec=pltpu.PrefetchScalarGridSpec(
            num_scalar_prefetch=4,
            grid=(T // tt,),
            in_specs=[
                pl.BlockSpec(memory_space=pl.ANY),
                pl.BlockSpec((tt, D_MODEL), lambda i, *_: (i, 0)),
            ],
            out_specs=pl.BlockSpec(memory_space=pl.ANY),
            scratch_shapes=[
                pltpu.SMEM((2, SUBLANES, tt), I32),
                pltpu.VMEM((TM_MOE, D_MODEL), F32),
                pltpu.SemaphoreType.DMA((2,)),
                pltpu.SemaphoreType.DMA,
                pltpu.SemaphoreType.DMA,
            ],
        ),
        out_shape=jax.ShapeDtypeStruct((n_rows, D_MODEL), F32),
        compiler_params=_cparams(("arbitrary",)),
        name="dispatch",
    )(counts, pstarts, padded, nb_used, dest_tiles, xn)


def _experts_kernel(be_ref, nb_ref, xs_ref, wgu_ref, bgu_ref, wd_ref, bd_ref, ys_ref,
                    wgu_bf, wd_bf, acc_ref):
    b = pl.program_id(0)

    @pl.when(b < nb_ref[0])
    def _():
        prev = be_ref[jnp.maximum(b - 1, 0)]

        @pl.when((b == 0) | (be_ref[b] != prev))
        def _():
            wgu_bf[...] = wgu_ref[0].astype(BF16)
            wd_bf[...] = wd_ref[0].astype(BF16)

        x = xs_ref[...].astype(BF16)
        for c in range(D_FF // FF_CHUNK):
            gcols = slice(c * FF_CHUNK, (c + 1) * FF_CHUNK)
            ucols = slice(D_FF + c * FF_CHUNK, D_FF + (c + 1) * FF_CHUNK)
            gate = jnp.dot(x, wgu_bf[:, gcols], preferred_element_type=F32) + bgu_ref[0, :, gcols]
            up = jnp.dot(x, wgu_bf[:, ucols], preferred_element_type=F32) + bgu_ref[0, :, ucols]
            gate = jnp.minimum(gate, SWIGLU_LIMIT)
            up = jnp.clip(up, -SWIGLU_LIMIT, SWIGLU_LIMIT)
            hdn = ((up + 1.0) * gate * jax.nn.sigmoid(gate * SWIGLU_ALPHA)).astype(BF16)
            part = jnp.dot(hdn, wd_bf[gcols, :], preferred_element_type=F32)
            if c == 0:
                acc_ref[...] = part
            else:
                acc_ref[...] += part
        ys_ref[...] = acc_ref[...] + bd_ref[0]

    @pl.when(b >= nb_ref[0])
    def _():
        ys_ref[...] = jnp.zeros_like(ys_ref)


def _experts(block_e, nb_used, xs, wgu, bgu, wd, bd):
    P = xs.shape[0]
    tm = TM_MOE
    nb = P // tm

    def row_block(b, be, nbu):
        return (jnp.minimum(b, nbu[0] - 1), 0)

    def out_block(b, be, nbu):
        return (b, 0)

    def expert_block(b, be, nbu):
        return (be[jnp.minimum(b, nbu[0] - 1)], 0, 0)

    return pl.pallas_call(
        _experts_kernel,
        grid_spec=pltpu.PrefetchScalarGridSpec(
            num_scalar_prefetch=2,
            grid=(nb,),
            in_specs=[
                pl.BlockSpec((tm, D_MODEL), row_block),
                pl.BlockSpec((1, D_MODEL, 2 * D_FF), expert_block),
                pl.BlockSpec((1, 1, 2 * D_FF), expert_block),
                pl.BlockSpec((1, D_FF, D_MODEL), expert_block),
                pl.BlockSpec((1, 1, D_MODEL), expert_block),
            ],
            out_specs=pl.BlockSpec((tm, D_MODEL), out_block),
            scratch_shapes=[
                pltpu.VMEM((D_MODEL, 2 * D_FF), BF16),
                pltpu.VMEM((D_FF, D_MODEL), BF16),
                pltpu.VMEM((tm, D_MODEL), F32),
            ],
        ),
        out_shape=jax.ShapeDtypeStruct((P, D_MODEL), F32),
        compiler_params=_cparams(("arbitrary",)),
        name="experts",
    )(block_e, nb_used, xs, wgu, bgu, wd, bd)


def _combine_kernel(dest_hbm, h1_ref, gates_ref, fw_ref, ys_hbm, out_ref,
                    dsm, buf, sem_d, sem_g):
    tt = h1_ref.shape[0]
    i = pl.program_id(0)
    n = pl.num_programs(0)
    slot = i % 2

    def dest_copy(j, sl):
        return pltpu.make_async_copy(dest_hbm.at[j], dsm.at[sl], sem_d.at[sl])

    def issue_gathers(sl):
        def per_token(t, carry):
            for k in range(TOP_K):
                d = dsm[sl, k, t]
                pltpu.make_async_copy(ys_hbm.at[pl.ds(d, 1), :],
                                      buf.at[sl, k, pl.ds(t, 1), :], sem_g.at[sl]).start()
            return carry

        lax.fori_loop(0, tt, per_token, 0, unroll=8)

    @pl.when(i == 0)
    def _():
        dest_copy(0, 0).start()
        dest_copy(0, 0).wait()
        issue_gathers(0)

        @pl.when(n > 1)
        def _():
            dest_copy(1, 1).start()

    @pl.when(i + 1 < n)
    def _():
        dest_copy(i + 1, 1 - slot).wait()
        issue_gathers(1 - slot)

    @pl.when(i + 2 < n)
    def _():
        dest_copy(i + 2, slot).start()

    for k in range(TOP_K):
        pltpu.make_async_copy(ys_hbm.at[pl.ds(0, tt), :], buf.at[slot, k], sem_g.at[slot]).wait()

    g_pad = jnp.concatenate([gates_ref[...], jnp.zeros((LANES - SUBLANES, tt), F32)], axis=0)
    g_t = g_pad.T
    acc = h1_ref[...]
    for k in range(TOP_K):
        acc = acc + g_t[:, k:k + 1] * buf[slot, k]
    out_ref[...] = _rms(acc, fw_ref[...])


def _combine(dest_tiles, h1, gates, fw, ys):
    T = h1.shape[0]
    tt = TT_ROWS
    return pl.pallas_call(
        _combine_kernel,
        grid=(T // tt,),
        in_specs=[
            pl.BlockSpec(memory_space=pl.ANY),
            pl.BlockSpec((tt, D_MODEL), lambda i: (i, 0)),
            pl.BlockSpec((SUBLANES, tt), lambda i: (0, i)),
            pl.BlockSpec((1, D_MODEL), lambda i: (0, 0)),
            pl.BlockSpec(memory_space=pl.ANY),
        ],
        out_specs=pl.BlockSpec((tt, D_MODEL), lambda i: (i, 0)),
        out_shape=jax.ShapeDtypeStruct((T, D_MODEL), F32),
        scratch_shapes=[
            pltpu.SMEM((2, SUBLANES, tt), I32),
            pltpu.VMEM((2, TOP_K, tt, D_MODEL), F32),
            pltpu.SemaphoreType.DMA((2,)),
            pltpu.SemaphoreType.DMA((2,)),
        ],
        compiler_params=_cparams(("arbitrary",)),
        name="combine",
    )(dest_tiles, h1, gates, fw, ys)


def kernel(x, norm1_w, w_in, mlstm_gate_b, mlstm_norm_w, conv_w, conv_b, lru_wr, lru_br, lru_wi,
           lru_bi, lru_lambda, lru_norm_w, w_out, norm2_w, router_w, router_b, moe_w_gu, moe_b_gu,
           moe_w_down, moe_b_down, final_norm_w):
    B, S, D = x.shape
    assert D == D_MODEL and norm1_w.shape[0] == 1
    assert S % TS_SEQ == 0 and (B * S) % TN_ROUTE == 0
    T = B * S
    x2 = x.reshape(T, D)

    w = w_in[0]
    g_off = 3 * M_WIDTH
    o_off = g_off + 2 * M_HEADS
    w_main = jnp.concatenate([w[:, :g_off], w[:, o_off:]], axis=1).astype(BF16)
    w_gate = jnp.zeros((D, LANES), F32)
    w_gate = w_gate.at[:, 0:M_HEADS].set(w[:, g_off:g_off + M_HEADS])
    w_gate = w_gate.at[:, SUBLANES:SUBLANES + M_HEADS].set(w[:, g_off + M_HEADS:o_off])
    gate_b = jnp.zeros((LANES, 1), F32)
    gate_b = gate_b.at[0:M_HEADS, 0].set(mlstm_gate_b[0, :M_HEADS])
    gate_b = gate_b.at[SUBLANES:SUBLANES + M_HEADS, 0].set(mlstm_gate_b[0, M_HEADS:])
    wri = jnp.concatenate([lru_wr[0], lru_wi[0]], axis=-1).astype(BF16)
    bri = jnp.concatenate([lru_br[0].reshape(R_HEADS, 1, R_BLOCK),
                           lru_bi[0].reshape(R_HEADS, 1, R_BLOCK)], axis=-1)
    rw = jnp.zeros((D, LANES), F32).at[:, :N_EXPERTS].set(router_w[0])
    rwh = rw.astype(BF16)
    rwl = (rw - rwh.astype(F32)).astype(BF16)

    proj, ig, fg = _inproj(x2, norm1_w, w_main, w_gate.astype(BF16), gate_b)
    h_m = _mlstm(proj, ig, fg, mlstm_norm_w, B, S)
    h_r = _rglru(proj, conv_w[0], conv_b, wri, bri, lru_lambda, lru_norm_w, B, S)
    h1, xn2, logits_t = _outproj(h_m, h_r, x2, w_out[0].astype(BF16), norm2_w, rwh, rwl,
                                 router_b[0].reshape(N_EXPERTS, 1))

    ids, gates, rank, cnt = _route(logits_t)
    counts = cnt[:, 0].astype(I32)
    padded = ((counts + TM_MOE - 1) // TM_MOE) * TM_MOE
    pends = jnp.cumsum(padded)
    pstarts = pends - padded
    nb = (T * TOP_K) // TM_MOE + N_EXPERTS
    nb_used = (pends[-1:] // TM_MOE).astype(I32)
    block_e = jnp.clip(jnp.searchsorted(pends, jnp.arange(nb, dtype=I32) * TM_MOE, side='right'),
                       0, N_EXPERTS - 1).astype(I32)

    dest_tiles = _dest(pstarts, ids, rank)
    xs = _dispatch(counts, pstarts, padded, nb_used, dest_tiles, xn2, nb * TM_MOE)
    ys = _experts(block_e, nb_used, xs, moe_w_gu[0], moe_b_gu[0].reshape(N_EXPERTS, 1, 2 * D_FF),
                  moe_w_down[0], moe_b_down[0].reshape(N_EXPERTS, 1, D))
    out = _combine(dest_tiles, h1, gates, final_norm_w.reshape(1, D), ys)
    return out.reshape(B, S, D)
```

```python
import jax
import jax.numpy as jnp
from jax import lax
from jax.experimental import pallas as pl
from jax.experimental.pallas import tpu as pltpu

F32 = jnp.float32
BF16 = jnp.bfloat16
I32 = jnp.int32

D_MODEL = 1024
M_HEADS = 4
HEAD_DIM = 128
M_WIDTH = M_HEADS * HEAD_DIM
R_HEADS = 4
R_BLOCK = 128
R_WIDTH = R_HEADS * R_BLOCK
CONV_WIDTH = 4
LRU_C = 8.0
N_EXPERTS = 32
TOP_K = 4
D_FF = 1024
SWIGLU_LIMIT = 7.0
SWIGLU_ALPHA = 1.702
EPS = 1e-6

N_MAIN = 5 * 512
LANES = 128
SUBLANES = 8
VMEM_LIMIT = 56 * 1024 * 1024

TM_PROJ = 512
TS_SEQ = 512
CHUNK = 128
TN_ROUTE = 2048
TT_ROWS = 256
TT_COMBINE = 512
TM_MOE = 512
FF_CHUNK = 512
META_ROWS = 16
D_EXT = D_MODEL + LANES
NEG_BIG = -1e30


def _cparams(sem, vmem=VMEM_LIMIT):
    return pltpu.CompilerParams(dimension_semantics=sem, vmem_limit_bytes=vmem)


def _rms(x, w):
    ms = jnp.mean(x * x, axis=-1, keepdims=True)
    return x * lax.rsqrt(ms + EPS) * w


def _sigmoid(x):
    return 0.5 * jnp.tanh(0.5 * x) + 0.5


def _segmented_scan(x, op, ident, lane_in_chunk):
    d = 1
    while d < CHUNK:
        shifted = pltpu.roll(x, d, axis=1)
        x = op(x, jnp.where(lane_in_chunk >= d, shifted, ident))
        d *= 2
    return x


def _inproj_kernel(x_ref, nw_ref, w_ref, wkt_ref, wg_ref, gb_ref,
                   proj_ref, kt_ref, grow_ref, rt_ref):
    tm = x_ref.shape[0]
    xn = _rms(x_ref[...], nw_ref[...]).astype(BF16)

    g = jnp.dot(xn, wg_ref[...], preferred_element_type=F32)
    gt = g.T + gb_ref[...]
    ig = gt[0:SUBLANES]
    fg = gt[SUBLANES:2 * SUBLANES]
    lf = jnp.minimum(fg, 0.0) - jnp.log1p(jnp.exp(-jnp.abs(fg)))
    lane_in_chunk = lax.broadcasted_iota(I32, (SUBLANES, tm), 1) % CHUNK
    b_row = _segmented_scan(lf, jnp.add, 0.0, lane_in_chunk)
    g_row = ig - b_row
    cm_row = _segmented_scan(g_row, jnp.maximum, NEG_BIG, lane_in_chunk)
    grow_ref[...] = g_row
    stacked = jnp.concatenate(
        [b_row, g_row, cm_row, jnp.zeros((LANES - 3 * SUBLANES, tm), F32)], axis=0)
    rt_ref[...] = stacked.T

    for c in range(N_MAIN // 512):
        cols = slice(c * 512, (c + 1) * 512)
        proj_ref[:, cols] = jnp.dot(xn, w_ref[:, cols], preferred_element_type=F32).astype(BF16)
    kt_ref[...] = lax.dot_general(wkt_ref[...], xn, (((1,), (1,)), ((), ())),
                                  preferred_element_type=F32).astype(BF16)


def _inproj(x2, nw, w_main, w_kt, w_gate, gate_b):
    T = x2.shape[0]
    tm = TM_PROJ

    def const(shape):
        return pl.BlockSpec(shape, lambda i: (0,) * len(shape))

    return pl.pallas_call(
        _inproj_kernel,
        grid=(T // tm,),
        in_specs=[
            pl.BlockSpec((tm, D_MODEL), lambda i: (i, 0)),
            const((1, D_MODEL)), const((D_MODEL, N_MAIN)), const((M_WIDTH, D_MODEL)),
            const((D_MODEL, LANES)), const((LANES, 1)),
        ],
        out_specs=[
            pl.BlockSpec((tm, N_MAIN), lambda i: (i, 0)),
            pl.BlockSpec((M_WIDTH, tm), lambda i: (0, i)),
            pl.BlockSpec((SUBLANES, tm), lambda i: (0, i)),
            pl.BlockSpec((tm, LANES), lambda i: (i, 0)),
        ],
        out_shape=[
            jax.ShapeDtypeStruct((T, N_MAIN), BF16),
            jax.ShapeDtypeStruct((M_WIDTH, T), BF16),
            jax.ShapeDtypeStruct((SUBLANES, T), F32),
            jax.ShapeDtypeStruct((T, LANES), F32),
        ],
        compiler_params=_cparams(("arbitrary",)),
        name="inproj",
    )(x2, nw, w_main, w_kt, w_gate, gate_b)


def _mlstm_kernel(q_ref, v_ref, o_ref, kt_ref, grow_ref, rt_ref, nw_ref, out_ref,
                  c_ref, m_ref, bc_ref):
    ts = q_ref.shape[0]
    scale = HEAD_DIM ** -0.5

    @pl.when(pl.program_id(1) == 0)
    def _():
        c_ref[...] = jnp.zeros_like(c_ref)
        m_ref[...] = jnp.zeros_like(m_ref)

    for term in range(3):
        for h in range(M_HEADS):
            col = term * SUBLANES + h
            bc_ref[term, h] = jnp.broadcast_to(rt_ref[:, col:col + 1], (ts, LANES))

    t_idx = lax.broadcasted_iota(I32, (CHUNK, CHUNK), 0)
    s_idx = lax.broadcasted_iota(I32, (CHUNK, CHUNK), 1)
    causal = t_idx >= s_idx
    ones_blk = jnp.ones((CHUNK, HEAD_DIM), BF16)

    for c in range(ts // CHUNK):
        rows = slice(c * CHUNK, (c + 1) * CHUNK)
        last = slice((c + 1) * CHUNK - 1, (c + 1) * CHUNK)
        for h in range(M_HEADS):
            cols = slice(h * HEAD_DIM, (h + 1) * HEAD_DIM)
            qc = q_ref[rows, cols]
            kt = kt_ref[cols, rows]
            v_ext = jnp.concatenate([v_ref[rows, cols], ones_blk], axis=1)
            b_bc = bc_ref[0, h, rows, :]
            cm_bc = bc_ref[2, h, rows, :]
            g_r = grow_ref[h:h + 1, rows]
            m_prev = m_ref[h:h + 1, :]
            c_prev = c_ref[h]

            mx = jnp.maximum(cm_bc, m_prev)
            decay_mat = jnp.exp(jnp.where(causal, g_r - mx, NEG_BIG))
            scores = jnp.dot(qc, kt, preferred_element_type=F32)
            p = (scores * decay_mat * scale).astype(BF16)
            intra = jnp.dot(p, v_ext, preferred_element_type=F32)
            inter = jnp.dot(qc, c_prev.astype(BF16), preferred_element_type=F32)
            w_inter = jnp.exp(m_prev - mx)
            num = intra[:, :HEAD_DIM] + w_inter * inter[:, :HEAD_DIM]
            den = intra[:, HEAD_DIM:] + w_inter * inter[:, HEAD_DIM:]
            m_t = b_bc + mx
            h_t = num / jnp.maximum(jnp.abs(den), jnp.exp(-m_t))

            mx_last = jnp.maximum(bc_ref[2, h, last, :], m_prev)
            w_row = jnp.exp(g_r - mx_last) * scale
            kw_t = (kt.astype(F32) * w_row).astype(BF16)
            upd = jnp.dot(kw_t, v_ext, preferred_element_type=F32)
            dec = jnp.exp(m_prev - mx_last)
            c_ref[h] = jnp.concatenate([dec, dec], axis=1) * c_prev + upd
            m_ref[h:h + 1, :] = bc_ref[0, h, last, :] + mx_last

            y = _sigmoid(o_ref[rows, cols].astype(F32)) * h_t
            y2 = y * y
            y2_hi = y2.astype(BF16)
            y2_lo = (y2 - y2_hi.astype(F32)).astype(BF16)
            ss = (jnp.dot(y2_hi, ones_blk, preferred_element_type=F32)
                  + jnp.dot(y2_lo, ones_blk, preferred_element_type=F32))
            out_ref[rows, cols] = (y * lax.rsqrt(ss * (1.0 / HEAD_DIM) + EPS)
                                   * nw_ref[:, cols]).astype(BF16)


def _mlstm(proj, kt, g_row, rt, nw, B, S):
    T = B * S
    ts = TS_SEQ
    nst = S // ts

    def col(j):
        return pl.BlockSpec((ts, M_WIDTH), lambda b, s, j=j: (b * nst + s, j))

    return pl.pallas_call(
        _mlstm_kernel,
        grid=(B, nst),
        in_specs=[col(0), col(1), col(2),
                  pl.BlockSpec((M_WIDTH, ts), lambda b, s: (0, b * nst + s)),
                  pl.BlockSpec((SUBLANES, ts), lambda b, s: (0, b * nst + s)),
                  pl.BlockSpec((ts, LANES), lambda b, s: (b * nst + s, 0)),
                  pl.BlockSpec((1, M_WIDTH), lambda b, s: (0, 0))],
        out_specs=pl.BlockSpec((ts, M_WIDTH), lambda b, s: (b * nst + s, 0)),
        out_shape=jax.ShapeDtypeStruct((T, M_WIDTH), BF16),
        scratch_shapes=[
            pltpu.VMEM((M_HEADS, HEAD_DIM, 2 * HEAD_DIM), F32),
            pltpu.VMEM((SUBLANES, LANES), F32),
            pltpu.VMEM((3, M_HEADS, ts, LANES), F32),
        ],
        compiler_params=_cparams(("arbitrary", "arbitrary")),
        name="mlstm",
    )(proj, proj, proj, kt, g_row, rt, nw)


def _gelu_tanh(x):
    return 0.5 * x * (1.0 + jnp.tanh(0.7978845608028654 * (x + 0.044715 * (x * x * x))))


def _rglru_kernel(rx_ref, rg_ref, cw_ref, cb_ref, wri_ref, bri_ref, lam_ref, nw_ref, out_ref,
                  xpad_ref, a_ref, b_ref, hcar_ref):
    ts = rx_ref.shape[0]
    ng = ts // SUBLANES

    @pl.when(pl.program_id(1) == 0)
    def _():
        xpad_ref[0:SUBLANES, :] = jnp.zeros((SUBLANES, R_WIDTH), F32)
        hcar_ref[...] = jnp.zeros_like(hcar_ref)

    x = rx_ref[...].astype(F32)
    xpad_ref[SUBLANES:SUBLANES + ts, :] = x
    xp = xpad_ref[...]
    xc = cb_ref[...]
    for j in range(CONV_WIDTH):
        sh = CONV_WIDTH - 1 - j
        tap = xp if sh == 0 else pltpu.roll(xp, sh, axis=0)
        xc = xc + cw_ref[j:j + 1, :] * tap[SUBLANES:SUBLANES + ts, :]
    xpad_ref[0:SUBLANES, :] = x[ts - SUBLANES:ts, :]

    nlam = -lam_ref[...]
    softplus_nlam = jnp.maximum(nlam, 0.0) + jnp.log1p(jnp.exp(-jnp.abs(nlam)))
    row_in_group = lax.broadcasted_iota(I32, (ng, SUBLANES, R_BLOCK), 1)

    for h in range(R_HEADS):
        cols = slice(h * R_BLOCK, (h + 1) * R_BLOCK)
        xh = xc[:, cols]
        ri = jnp.dot(xh.astype(BF16), wri_ref[h], preferred_element_type=F32) + bri_ref[h]
        r = _sigmoid(ri[:, :R_BLOCK])
        i = _sigmoid(ri[:, R_BLOCK:])
        log_a = -LRU_C * r * softplus_nlam[:, cols]
        a = jnp.exp(log_a)
        one_m_a2 = -jnp.tanh(log_a) * (a * a + 1.0)
        bt = jnp.sqrt(one_m_a2) * (i * xh)
        a = a.reshape(ng, SUBLANES, R_BLOCK)
        bt = bt.reshape(ng, SUBLANES, R_BLOCK)
        d = 1
        while d < SUBLANES:
            keep = row_in_group >= d
            a_sh = pltpu.roll(a, d, axis=1)
            b_sh = pltpu.roll(bt, d, axis=1)
            bt = bt + jnp.where(keep, a * b_sh, 0.0)
            a = a * jnp.where(keep, a_sh, 1.0)
            d *= 2
        a_ref[:, cols] = a.reshape(ts, R_BLOCK)
        b_ref[:, cols] = bt.reshape(ts, R_BLOCK)

    def group_step(g, hprev):
        r0 = pl.multiple_of(g * SUBLANES, SUBLANES)
        hr = b_ref[pl.ds(r0, SUBLANES), :] + a_ref[pl.ds(r0, SUBLANES), :] * hprev
        b_ref[pl.ds(r0, SUBLANES), :] = hr
        return hr[SUBLANES - 1:SUBLANES, :]

    hcar_ref[...] = lax.fori_loop(0, ng, group_step, hcar_ref[...], unroll=4)

    y = _gelu_tanh(rg_ref[...].astype(F32)) * b_ref[...]
    out_ref[...] = _rms(y, nw_ref[...]).astype(BF16)


def _rglru(proj, cw, cb, wri, bri, lam, nw, B, S):
    T = B * S
    ts = TS_SEQ
    nst = S // ts

    def col(j):
        return pl.BlockSpec((ts, R_WIDTH), lambda b, s, j=j: (b * nst + s, j))

    def const(shape):
        return pl.BlockSpec(shape, lambda b, s: (0,) * len(shape))

    return pl.pallas_call(
        _rglru_kernel,
        grid=(B, nst),
        in_specs=[col(3), col(4), const((CONV_WIDTH, R_WIDTH)), const((1, R_WIDTH)),
                  const((R_HEADS, R_BLOCK, 2 * R_BLOCK)), const((R_HEADS, 1, 2 * R_BLOCK)),
                  const((1, R_WIDTH)), const((1, R_WIDTH))],
        out_specs=pl.BlockSpec((ts, R_WIDTH), lambda b, s: (b * nst + s, 0)),
        out_shape=jax.ShapeDtypeStruct((T, R_WIDTH), BF16),
        scratch_shapes=[
            pltpu.VMEM((ts + SUBLANES, R_WIDTH), F32),
            pltpu.VMEM((ts, R_WIDTH), F32),
            pltpu.VMEM((ts, R_WIDTH), F32),
            pltpu.VMEM((1, R_WIDTH), F32),
        ],
        compiler_params=_cparams(("arbitrary", "arbitrary")),
        name="rglru",
    )(proj, proj, cw, cb, wri, bri, lam, nw)


def _outproj_kernel(hm_ref, hr_ref, x_ref, wo_ref, n2_ref, rw2_ref, rb_ref,
                    h1_ref, xn_ref, lg_ref):
    mix = jnp.dot(hm_ref[...], wo_ref[0:M_WIDTH, :], preferred_element_type=F32)
    mix = mix + jnp.dot(hr_ref[...], wo_ref[M_WIDTH:, :], preferred_element_type=F32)
    h1 = x_ref[...] + mix
    h1_ref[...] = h1
    xn = _rms(h1, n2_ref[...])
    xn_ref[...] = xn
    xh = xn.astype(BF16)
    xl = (xn - xh.astype(F32)).astype(BF16)
    lg2 = jnp.dot(xh, rw2_ref[...], preferred_element_type=F32)
    lg = lg2[:, :LANES] + lg2[:, LANES:]
    lg = lg + jnp.dot(xl, rw2_ref[:, :LANES], preferred_element_type=F32)
    lg_ref[...] = lg.T[0:N_EXPERTS] + rb_ref[...]


def _outproj(hm, hr, x2, wo, n2, rw2, rb):
    T = x2.shape[0]
    tm = TM_PROJ

    def const(shape):
        return pl.BlockSpec(shape, lambda i: (0,) * len(shape))

    return pl.pallas_call(
        _outproj_kernel,
        grid=(T // tm,),
        in_specs=[
            pl.BlockSpec((tm, M_WIDTH), lambda i: (i, 0)),
            pl.BlockSpec((tm, R_WIDTH), lambda i: (i, 0)),
            pl.BlockSpec((tm, D_MODEL), lambda i: (i, 0)),
            const((D_MODEL, D_MODEL)), const((1, D_MODEL)),
            const((D_MODEL, 2 * LANES)), const((N_EXPERTS, 1)),
        ],
        out_specs=[
            pl.BlockSpec((tm, D_MODEL), lambda i: (i, 0)),
            pl.BlockSpec((tm, D_MODEL), lambda i: (i, 0)),
            pl.BlockSpec((N_EXPERTS, tm), lambda i: (0, i)),
        ],
        out_shape=[
            jax.ShapeDtypeStruct((T, D_MODEL), F32),
            jax.ShapeDtypeStruct((T, D_MODEL), F32),
            jax.ShapeDtypeStruct((N_EXPERTS, T), F32),
        ],
        compiler_params=_cparams(("arbitrary",)),
        name="outproj",
    )(hm, hr, x2, wo, n2, rw2, rb)


def _route_kernel(lg_ref, ids_ref, rank_ref, meta_ref, cnt_ref, carry_ref):
    tn = lg_ref.shape[1]
    sub = 256

    @pl.when(pl.program_id(0) == 0)
    def _():
        carry_ref[...] = jnp.zeros_like(carry_ref)

    logits = lg_ref[...]
    e_idx = lax.broadcasted_iota(I32, (N_EXPERTS, tn), 0)
    vals, idxs, sels = [], [], []
    for _ in range(TOP_K):
        m = jnp.max(logits, axis=0, keepdims=True)
        idx = jnp.min(jnp.where(logits == m, e_idx, N_EXPERTS), axis=0, keepdims=True)
        sel = e_idx == idx
        logits = jnp.where(sel, -jnp.inf, logits)
        vals.append(m)
        idxs.append(idx)
        sels.append(sel)
    exps = [jnp.exp(v - vals[0]) for v in vals]
    denom = exps[0] + exps[1] + exps[2] + exps[3]
    chosen = (sels[0] | sels[1] | sels[2] | sels[3]).astype(BF16)

    s_i = lax.broadcasted_iota(I32, (sub, sub), 0)
    t_i = lax.broadcasted_iota(I32, (sub, sub), 1)
    upper = (s_i < t_i).astype(BF16)
    carry = carry_ref[:, 0:1]
    parts = []
    for c in range(tn // sub):
        mc = chosen[:, c * sub:(c + 1) * sub]
        cs = jnp.dot(mc, upper, preferred_element_type=F32) + carry
        parts.append(cs)
        carry = cs[:, sub - 1:sub] + mc[:, sub - 1:sub].astype(F32)
    before = jnp.concatenate(parts, axis=1)
    carry_ref[...] = jnp.broadcast_to(carry, carry_ref.shape)
    cnt_ref[...] = jnp.broadcast_to(carry, cnt_ref.shape)

    zeros_i = jnp.zeros((SUBLANES - TOP_K, tn), I32)
    ranks = [jnp.sum(jnp.where(s, before, 0.0), axis=0, keepdims=True).astype(I32) for s in sels]
    ids_ref[...] = jnp.concatenate(idxs + [zeros_i], axis=0)
    rank_ref[...] = jnp.concatenate(ranks + [zeros_i], axis=0)
    tok = (pl.program_id(0) * tn + lax.broadcasted_iota(I32, (1, tn), 1)).astype(F32)
    meta_ref[...] = jnp.concatenate(
        [tok] + [i.astype(F32) for i in idxs] + [e / denom for e in exps]
        + [jnp.ones((1, tn), F32), jnp.zeros((META_ROWS - 2 - 2 * TOP_K, tn), F32)], axis=0)


def _route(logits_t):
    T = logits_t.shape[1]
    tn = TN_ROUTE
    tok = pl.BlockSpec((SUBLANES, tn), lambda i: (0, i))
    return pl.pallas_call(
        _route_kernel,
        grid=(T // tn,),
        in_specs=[pl.BlockSpec((N_EXPERTS, tn), lambda i: (0, i))],
        out_specs=[tok, tok, pl.BlockSpec((META_ROWS, tn), lambda i: (0, i)),
                   pl.BlockSpec((N_EXPERTS, LANES), lambda i: (0, 0))],
        out_shape=[
            jax.ShapeDtypeStruct((SUBLANES, T), I32),
            jax.ShapeDtypeStruct((SUBLANES, T), I32),
            jax.ShapeDtypeStruct((META_ROWS, T), F32),
            jax.ShapeDtypeStruct((N_EXPERTS, LANES), F32),
        ],
        scratch_shapes=[pltpu.VMEM((N_EXPERTS, LANES), F32)],
        compiler_params=_cparams(("arbitrary",)),
        name="route",
    )(logits_t)


def _dest_kernel(pst_ref, ids_ref, rank_ref, dest_ref):
    ids = ids_ref[...]
    start = jnp.zeros(ids.shape, I32)
    for e in range(N_EXPERTS):
        start = jnp.where(ids == e, pst_ref[e], start)
    dest = rank_ref[...] + start
    for j in range(dest_ref.shape[0]):
        dest_ref[j] = dest[:, j * TT_ROWS:(j + 1) * TT_ROWS]


def _dest(pstarts, ids, rank):
    T = ids.shape[1]
    tn = TN_ROUTE
    per = tn // TT_ROWS
    tok = pl.BlockSpec((SUBLANES, tn), lambda i, pst: (0, i))
    return pl.pallas_call(
        _dest_kernel,
        grid_spec=pltpu.PrefetchScalarGridSpec(
            num_scalar_prefetch=1,
            grid=(T // tn,),
            in_specs=[tok, tok],
            out_specs=pl.BlockSpec((per, SUBLANES, TT_ROWS), lambda i, pst: (i, 0, 0)),
        ),
        out_shape=jax.ShapeDtypeStruct((T // TT_ROWS, SUBLANES, TT_ROWS), I32),
        compiler_params=_cparams(("arbitrary",)),
        name="dest",
    )(pstarts, ids, rank)


def _dispatch_kernel(cnt_ref, pst_ref, pad_ref, nb_ref, dest_hbm, x_ref, meta_ref, xs_hbm,
                     dsm, ext, zbuf, sem_d, sem_x, sem_z):
    tt = x_ref.shape[0]
    i = pl.program_id(0)
    n = pl.num_programs(0)
    slot = i % 2
    n_blocks = xs_hbm.shape[0] // TM_MOE

    def dest_copy(j, sl):
        return pltpu.make_async_copy(dest_hbm.at[j], dsm.at[sl], sem_d.at[sl])

    def zero_copy(r):
        return pltpu.make_async_copy(zbuf.at[pl.ds(0, 1), :], xs_hbm.at[pl.ds(r, 1), :], sem_z)

    def zero_block_copy(b):
        r0 = pl.multiple_of(b * TM_MOE, TM_MOE)
        return pltpu.make_async_copy(zbuf, xs_hbm.at[pl.ds(r0, TM_MOE), :], sem_z)

    def rows_wait(sl):
        for _ in range(TOP_K):
            pltpu.make_async_copy(ext.at[sl], xs_hbm.at[pl.ds(0, tt), :], sem_x.at[sl]).wait()

    def for_each_pad_row(fn):
        def per_expert(e, carry):
            lo = pst_ref[e] + cnt_ref[e]
            hi = pst_ref[e] + pad_ref[e]

            def per_row(r, c2):
                fn(r)
                return c2

            return lax.fori_loop(lo, hi, per_row, carry)

        lax.fori_loop(0, N_EXPERTS, per_expert, 0)

    def for_each_unused_block(fn):
        def per_block(b, carry):
            fn(b)
            return carry

        lax.fori_loop(nb_ref[0], n_blocks, per_block, 0)

    @pl.when(i == 0)
    def _():
        dest_copy(0, 0).start()
        zbuf[...] = jnp.zeros_like(zbuf)
        for_each_pad_row(lambda r: zero_copy(r).start())
        for_each_unused_block(lambda b: zero_block_copy(b).start())

    ext[slot, :, 0:D_MODEL] = x_ref[...]
    meta_pad = jnp.concatenate([meta_ref[...], jnp.zeros((LANES - META_ROWS, tt), F32)], axis=0)
    ext[slot, :, D_MODEL:D_EXT] = meta_pad.T

    dest_copy(i, slot).wait()

    @pl.when(i + 1 < n)
    def _():
        dest_copy(i + 1, 1 - slot).start()

    def per_token(t, carry):
        for k in range(TOP_K):
            d = dsm[slot, k, t]
            pltpu.make_async_copy(ext.at[slot, pl.ds(t, 1), :], xs_hbm.at[pl.ds(d, 1), :],
                                  sem_x.at[slot]).start(priority=k % 2)
        return carry

    lax.fori_loop(0, tt, per_token, 0, unroll=8)

    @pl.when(i == 0)
    def _():
        for_each_pad_row(lambda r: zero_copy(r).wait())
        for_each_unused_block(lambda b: zero_block_copy(b).wait())

    @pl.when(i > 0)
    def _():
        rows_wait(1 - slot)

    @pl.when(i == n - 1)
    def _():
        rows_wait(slot)


def _dispatch(counts, pstarts, padded, nb_used, dest_tiles, xn, meta, n_rows):
    T = xn.shape[0]
    tt = TT_ROWS
    return pl.pallas_call(
        _dispatch_kernel,
        grid_spec=pltpu.PrefetchScalarGridSpec(
            num_scalar_prefetch=4,
            grid=(T // tt,),
            in_specs=[
                pl.BlockSpec(memory_space=pl.ANY),
                pl.BlockSpec((tt, D_MODEL), lambda i, *_: (i, 0)),
                pl.BlockSpec((META_ROWS, tt), lambda i, *_: (0, i)),
            ],
            out_specs=pl.BlockSpec(memory_space=pl.ANY),
            scratch_shapes=[
                pltpu.SMEM((2, SUBLANES, tt), I32),
                pltpu.VMEM((2, tt, D_EXT), F32),
                pltpu.VMEM((TM_MOE, D_EXT), F32),
                pltpu.SemaphoreType.DMA((2,)),
                pltpu.SemaphoreType.DMA((2,)),
                pltpu.SemaphoreType.DMA,
            ],
        ),
        out_shape=jax.ShapeDtypeStruct((n_rows, D_EXT), F32),
        compiler_params=_cparams(("arbitrary",)),
        name="dispatch",
    )(counts, pstarts, padded, nb_used, dest_tiles, xn, meta)


def _experts_kernel(be_ref, nb_ref, xs_ref, wgu_ref, bgu_ref, wd_ref, bd_ref, out_hbm,
                    wgu_bf, wd_bf, acc_ref, ybuf, code_v, code_s, sem_c, sem_y):
    tm = TM_MOE
    b = pl.program_id(0)
    n = pl.num_programs(0)
    slot = b % 2
    nbu = nb_ref[0]
    n_tok = (out_hbm.shape[0] - 2 * tm) // TOP_K
    dump0 = TOP_K * n_tok

    def scatter_wait(sl):
        pltpu.make_async_copy(ybuf.at[sl], out_hbm.at[pl.ds(0, tm), :], sem_y.at[sl]).wait()

    def code_copy(sl):
        return pltpu.make_async_copy(code_v, code_s.at[sl], sem_c)

    @pl.when(b == 0)
    def _():
        ybuf[...] = jnp.zeros_like(ybuf)
        for sl in range(2):
            cp = pltpu.make_async_copy(ybuf.at[sl], out_hbm.at[pl.ds(dump0 + sl * tm, tm), :],
                                       sem_y.at[sl])
            cp.start()
            cp.wait()

    @pl.when(b < nbu)
    def _():
        e = be_ref[b]
        prev = be_ref[jnp.maximum(b - 1, 0)]

        @pl.when((b == 0) | (e != prev))
        def _():
            wgu_bf[...] = wgu_ref[0].astype(BF16)
            wd_bf[...] = wd_ref[0].astype(BF16)

        meta = xs_ref[:, D_MODEL:D_EXT]
        e_f = e.astype(F32)
        tok = meta[:, 0:1]
        valid = meta[:, 1 + 2 * TOP_K:2 + 2 * TOP_K] > 0.5
        choice = jnp.zeros((tm, 1), F32)
        gate = jnp.zeros((tm, 1), F32)
        for j in range(TOP_K):
            hit = meta[:, 1 + j:2 + j] == e_f
            choice = jnp.where(hit, float(j), choice)
            gate = jnp.where(hit, meta[:, 1 + TOP_K + j:2 + TOP_K + j], gate)
        spare = (dump0 + slot * tm + lax.broadcasted_iota(I32, (tm, 1), 0)).astype(F32)
        code = jnp.where(valid, choice * float(n_tok) + tok, spare)
        code_row = jnp.broadcast_to(code, (tm, LANES)).T[0:1, :]
        code_v[...] = code_row.astype(I32)
        code_copy(slot).start()

        x = xs_ref[:, 0:D_MODEL].astype(BF16)
        for c in range(D_FF // FF_CHUNK):
            gcols = slice(c * FF_CHUNK, (c + 1) * FF_CHUNK)
            ucols = slice(D_FF + c * FF_CHUNK, D_FF + (c + 1) * FF_CHUNK)
            g = jnp.dot(x, wgu_bf[:, gcols], preferred_element_type=F32) + bgu_ref[0, :, gcols]
            up = jnp.dot(x, wgu_bf[:, ucols], preferred_element_type=F32) + bgu_ref[0, :, ucols]
            g = jnp.minimum(g, SWIGLU_LIMIT)
            up = jnp.clip(up, -SWIGLU_LIMIT, SWIGLU_LIMIT)
            hdn = ((up + 1.0) * g * jax.nn.sigmoid(g * SWIGLU_ALPHA)).astype(BF16)
            part = jnp.dot(hdn, wd_bf[gcols, :], preferred_element_type=F32)
            if c == 0:
                acc_ref[...] = part
            else:
                acc_ref[...] += part
        ybuf[slot] = (acc_ref[...] + bd_ref[0]) * gate

        code_copy(slot).wait()

        def per_row(r, carry):
            dst = code_s[slot, 0, r]
            pltpu.make_async_copy(ybuf.at[slot, pl.ds(r, 1), :], out_hbm.at[pl.ds(dst, 1), :],
                                  sem_y.at[slot]).start()
            return carry

        lax.fori_loop(0, tm, per_row, 0, unroll=8)

        @pl.when(b > 0)
        def _():
            scatter_wait(1 - slot)

        @pl.when(b == n - 1)
        def _():
            scatter_wait(slot)

    @pl.when((b == nbu) & (b > 0))
    def _():
        scatter_wait(1 - slot)


def _experts(block_e, nb_used, xs, wgu, bgu, wd, bd, n_tok):
    P = xs.shape[0]
    tm = TM_MOE
    nb = P // tm

    def row_block(b, be, nbu):
        return (jnp.minimum(b, nbu[0] - 1), 0)

    def expert_block(b, be, nbu):
        return (be[jnp.minimum(b, nbu[0] - 1)], 0, 0)

    return pl.pallas_call(
        _experts_kernel,
        grid_spec=pltpu.PrefetchScalarGridSpec(
            num_scalar_prefetch=2,
            grid=(nb,),
            in_specs=[
                pl.BlockSpec((tm, D_EXT), row_block),
                pl.BlockSpec((1, D_MODEL, 2 * D_FF), expert_block),
                pl.BlockSpec((1, 1, 2 * D_FF), expert_block),
                pl.BlockSpec((1, D_FF, D_MODEL), expert_block),
                pl.BlockSpec((1, 1, D_MODEL), expert_block),
            ],
            out_specs=pl.BlockSpec(memory_space=pl.ANY),
            scratch_shapes=[
                pltpu.VMEM((D_MODEL, 2 * D_FF), BF16),
                pltpu.VMEM((D_FF, D_MODEL), BF16),
                pltpu.VMEM((tm, D_MODEL), F32),
                pltpu.VMEM((2, tm, D_MODEL), F32),
                pltpu.VMEM((1, tm), I32),
                pltpu.SMEM((2, 1, tm), I32),
                pltpu.SemaphoreType.DMA,
                pltpu.SemaphoreType.DMA((2,)),
            ],
        ),
        out_shape=jax.ShapeDtypeStruct((TOP_K * n_tok + 2 * tm, D_MODEL), F32),
        compiler_params=_cparams(("arbitrary",)),
        name="experts",
    )(block_e, nb_used, xs, wgu, bgu, wd, bd)


def _combine_kernel(h1_ref, p0_ref, p1_ref, p2_ref, p3_ref, fw_ref, out_ref):
    acc = h1_ref[...] + (p0_ref[...] + p1_ref[...] + p2_ref[...] + p3_ref[...])
    out_ref[...] = _rms(acc, fw_ref[...])


def _combine(h1, planes, fw):
    T = h1.shape[0]
    tt = TT_COMBINE
    nt = T // tt

    def plane(k):
        return pl.BlockSpec((tt, D_MODEL), lambda i, k=k: (k * nt + i, 0))

    return pl.pallas_call(
        _combine_kernel,
        grid=(nt,),
        in_specs=[pl.BlockSpec((tt, D_MODEL), lambda i: (i, 0)),
                  plane(0), plane(1), plane(2), plane(3),
                  pl.BlockSpec((1, D_MODEL), lambda i: (0, 0))],
        out_specs=pl.BlockSpec((tt, D_MODEL), lambda i: (i, 0)),
        out_shape=jax.ShapeDtypeStruct((T, D_MODEL), F32),
        compiler_params=_cparams(("arbitrary",)),
        name="combine",
    )(h1, planes, planes, planes, planes, fw)


def kernel(x, norm1_w, w_in, mlstm_gate_b, mlstm_norm_w, conv_w, conv_b, lru_wr, lru_br, lru_wi,
           lru_bi, lru_lambda, lru_norm_w, w_out, norm2_w, router_w, router_b, moe_w_gu, moe_b_gu,
           moe_w_down, moe_b_down, final_norm_w):
    B, S, D = x.shape
    assert D == D_MODEL and norm1_w.shape[0] == 1
    assert S % TS_SEQ == 0 and (B * S) % TN_ROUTE == 0
    T = B * S
    x2 = x.reshape(T, D)

    w = w_in[0]
    k_off = M_WIDTH
    v_off = 2 * M_WIDTH
    g_off = 3 * M_WIDTH
    o_off = g_off + 2 * M_HEADS
    w_main = jnp.concatenate([w[:, :k_off], w[:, v_off:g_off], w[:, o_off:]], axis=1).astype(BF16)
    w_kt = w[:, k_off:v_off].T.astype(BF16)
    w_gate = jnp.zeros((D, LANES), F32)
    w_gate = w_gate.at[:, 0:M_HEADS].set(w[:, g_off:g_off + M_HEADS])
    w_gate = w_gate.at[:, SUBLANES:SUBLANES + M_HEADS].set(w[:, g_off + M_HEADS:o_off])
    gate_b = jnp.zeros((LANES, 1), F32)
    gate_b = gate_b.at[0:M_HEADS, 0].set(mlstm_gate_b[0, :M_HEADS])
    gate_b = gate_b.at[SUBLANES:SUBLANES + M_HEADS, 0].set(mlstm_gate_b[0, M_HEADS:])
    wri = jnp.concatenate([lru_wr[0], lru_wi[0]], axis=-1).astype(BF16)
    bri = jnp.concatenate([lru_br[0].reshape(R_HEADS, 1, R_BLOCK),
                           lru_bi[0].reshape(R_HEADS, 1, R_BLOCK)], axis=-1)
    rw = jnp.zeros((D, LANES), F32).at[:, :N_EXPERTS].set(router_w[0])
    rwh = rw.astype(BF16)
    rwl = (rw - rwh.astype(F32)).astype(BF16)
    rw2 = jnp.concatenate([rwh, rwl], axis=1)

    proj, kt, g_row, rt = _inproj(x2, norm1_w, w_main, w_kt, w_gate.astype(BF16), gate_b)
    h_m = _mlstm(proj, kt, g_row, rt, mlstm_norm_w, B, S)
    h_r = _rglru(proj, conv_w[0], conv_b, wri, bri, lru_lambda, lru_norm_w, B, S)
    h1, xn2, logits_t = _outproj(h_m, h_r, x2, w_out[0].astype(BF16), norm2_w, rw2,
                                 router_b[0].reshape(N_EXPERTS, 1))

    ids, rank, meta, cnt = _route(logits_t)
    counts = cnt[:, 0].astype(I32)
    padded = ((counts + TM_MOE - 1) // TM_MOE) * TM_MOE
    pends = jnp.cumsum(padded)
    pstarts = pends - padded
    nb = (T * TOP_K) // TM_MOE + N_EXPERTS
    nb_used = (pends[-1:] // TM_MOE).astype(I32)
    block_start = jnp.arange(nb, dtype=I32) * TM_MOE
    block_e = jnp.minimum(jnp.sum((pends[None, :] <= block_start[:, None]).astype(I32), axis=1),
                          N_EXPERTS - 1)

    dest_tiles = _dest(pstarts, ids, rank)
    xs = _dispatch(counts, pstarts, padded, nb_used, dest_tiles, xn2, meta, nb * TM_MOE)
    planes = _experts(block_e, nb_used, xs, moe_w_gu[0],
                      moe_b_gu[0].reshape(N_EXPERTS, 1, 2 * D_FF),
                      moe_w_down[0], moe_b_down[0].reshape(N_EXPERTS, 1, D), T)
    out = _combine(h1, planes, final_norm_w.reshape(1, D))
    return out.reshape(B, S, D)
```

```python
import jax
import jax.numpy as jnp
from jax import lax
from jax.experimental import pallas as pl
from jax.experimental.pallas import tpu as pltpu

F32 = jnp.float32
BF16 = jnp.bfloat16
I32 = jnp.int32

D_MODEL = 1024
M_HEADS = 4
HEAD_DIM = 128
M_WIDTH = M_HEADS * HEAD_DIM
R_HEADS = 4
R_BLOCK = 128
R_WIDTH = R_HEADS * R_BLOCK
CONV_WIDTH = 4
LRU_C = 8.0
N_EXPERTS = 32
TOP_K = 4
D_FF = 1024
SWIGLU_LIMIT = 7.0
SWIGLU_ALPHA = 1.702
EPS = 1e-6

N_MAIN = 5 * 512
LANES = 128
SUBLANES = 8
VMEM_LIMIT = 56 * 1024 * 1024

TM_PROJ = 512
TS_SEQ = 512
CHUNK = 128
TN_ROUTE = 2048
TT_ROWS = 256
TT_COMBINE = 512
TM_MOE = 512
FF_CHUNK = 512
META_ROWS = 16
NEG_BIG = -1e30
U32 = jnp.uint32


def _cparams(sem, vmem=VMEM_LIMIT):
    return pltpu.CompilerParams(dimension_semantics=sem, vmem_limit_bytes=vmem)


def _rms(x, w):
    ms = jnp.mean(x * x, axis=-1, keepdims=True)
    return x * lax.rsqrt(ms + EPS) * w


def _sigmoid(x):
    return 0.5 * jnp.tanh(0.5 * x) + 0.5


def _segmented_scan(x, op, ident, lane_in_chunk):
    d = 1
    while d < CHUNK:
        shifted = pltpu.roll(x, d, axis=1)
        x = op(x, jnp.where(lane_in_chunk >= d, shifted, ident))
        d *= 2
    return x


def _inproj_kernel(x_ref, nw_ref, w_ref, wkt_ref, wg_ref, gb_ref,
                   proj_ref, kt_ref, grow_ref, rt_ref):
    tm = x_ref.shape[0]
    xn = _rms(x_ref[...], nw_ref[...]).astype(BF16)

    g = jnp.dot(xn, wg_ref[...], preferred_element_type=F32)
    gt = g.T + gb_ref[...]
    ig = gt[0:SUBLANES]
    fg = gt[SUBLANES:2 * SUBLANES]
    lf = jnp.minimum(fg, 0.0) - jnp.log1p(jnp.exp(-jnp.abs(fg)))
    lane_in_chunk = lax.broadcasted_iota(I32, (SUBLANES, tm), 1) % CHUNK
    b_row = _segmented_scan(lf, jnp.add, 0.0, lane_in_chunk)
    g_row = ig - b_row
    cm_row = _segmented_scan(g_row, jnp.maximum, NEG_BIG, lane_in_chunk)
    grow_ref[...] = g_row
    stacked = jnp.concatenate(
        [b_row, g_row, cm_row, jnp.zeros((LANES - 3 * SUBLANES, tm), F32)], axis=0)
    rt_ref[...] = stacked.T

    for c in range(N_MAIN // 512):
        cols = slice(c * 512, (c + 1) * 512)
        proj_ref[:, cols] = jnp.dot(xn, w_ref[:, cols], preferred_element_type=F32).astype(BF16)

    kt_ref[...] = lax.dot_general(wkt_ref[...], xn, (((1,), (1,)), ((), ())),
                                  preferred_element_type=F32).astype(BF16)


def _inproj(x2, nw, w_main, w_kt, w_gate, gate_b):
    T = x2.shape[0]
    tm = TM_PROJ

    def const(shape):
        return pl.BlockSpec(shape, lambda i: (0,) * len(shape))

    return pl.pallas_call(
        _inproj_kernel,
        grid=(T // tm,),
        in_specs=[
            pl.BlockSpec((tm, D_MODEL), lambda i: (i, 0)),
            const((1, D_MODEL)), const((D_MODEL, N_MAIN)), const((M_WIDTH, D_MODEL)),
            const((D_MODEL, LANES)), const((LANES, 1)),
        ],
        out_specs=[
            pl.BlockSpec((tm, N_MAIN), lambda i: (i, 0)),
            pl.BlockSpec((M_WIDTH, tm), lambda i: (0, i)),
            pl.BlockSpec((SUBLANES, tm), lambda i: (0, i)),
            pl.BlockSpec((tm, LANES), lambda i: (i, 0)),
        ],
        out_shape=[
            jax.ShapeDtypeStruct((T, N_MAIN), BF16),
            jax.ShapeDtypeStruct((M_WIDTH, T), BF16),
            jax.ShapeDtypeStruct((SUBLANES, T), F32),
            jax.ShapeDtypeStruct((T, LANES), F32),
        ],
        compiler_params=_cparams(("arbitrary",)),
        name="inproj",
    )(x2, nw, w_main, w_kt, w_gate, gate_b)


def _mlstm_kernel(q_ref, v_ref, o_ref, kt_ref, grow_ref, rt_ref, nw_ref, out_ref,
                  c_ref, m_ref, bc_ref):
    ts = q_ref.shape[0]
    scale = HEAD_DIM ** -0.5

    @pl.when(pl.program_id(1) == 0)
    def _():
        c_ref[...] = jnp.zeros_like(c_ref)
        m_ref[...] = jnp.zeros_like(m_ref)

    for term in range(3):
        for h in range(M_HEADS):
            col = term * SUBLANES + h
            bc_ref[term, h] = jnp.broadcast_to(rt_ref[:, col:col + 1], (ts, LANES))

    t_idx = lax.broadcasted_iota(I32, (CHUNK, CHUNK), 0)
    s_idx = lax.broadcasted_iota(I32, (CHUNK, CHUNK), 1)
    causal = t_idx >= s_idx
    ones_blk = jnp.ones((CHUNK, HEAD_DIM), BF16)

    for c in range(ts // CHUNK):
        rows = slice(c * CHUNK, (c + 1) * CHUNK)
        last = slice((c + 1) * CHUNK - 1, (c + 1) * CHUNK)
        for h in range(M_HEADS):
            cols = slice(h * HEAD_DIM, (h + 1) * HEAD_DIM)
            qc = q_ref[rows, cols]
            kt = kt_ref[cols, rows]
            v_ext = jnp.concatenate([v_ref[rows, cols], ones_blk], axis=1)
            b_bc = bc_ref[0, h, rows, :]
            cm_bc = bc_ref[2, h, rows, :]
            g_r = grow_ref[h:h + 1, rows]
            m_prev = m_ref[h:h + 1, :]
            c_prev = c_ref[h]

            mx = jnp.maximum(cm_bc, m_prev)
            decay_mat = jnp.exp(jnp.where(causal, g_r - mx, NEG_BIG))
            scores = jnp.dot(qc, kt, preferred_element_type=F32)
            p = (scores * decay_mat * scale).astype(BF16)
            intra = jnp.dot(p, v_ext, preferred_element_type=F32)
            inter = jnp.dot(qc, c_prev.astype(BF16), preferred_element_type=F32)
            w_inter = jnp.exp(m_prev - mx)
            num = intra[:, :HEAD_DIM] + w_inter * inter[:, :HEAD_DIM]
            den = intra[:, HEAD_DIM:] + w_inter * inter[:, HEAD_DIM:]
            m_t = b_bc + mx
            h_t = num / jnp.maximum(jnp.abs(den), jnp.exp(-m_t))

            mx_last = jnp.maximum(bc_ref[2, h, last, :], m_prev)
            w_row = jnp.exp(g_r - mx_last) * scale
            kw_t = (kt.astype(F32) * w_row).astype(BF16)
            upd = jnp.dot(kw_t, v_ext, preferred_element_type=F32)
            dec = jnp.exp(m_prev - mx_last)
            c_ref[h] = jnp.concatenate([dec, dec], axis=1) * c_prev + upd
            m_ref[h:h + 1, :] = bc_ref[0, h, last, :] + mx_last

            y = _sigmoid(o_ref[rows, cols].astype(F32)) * h_t
            y2 = y * y
            y2_hi = y2.astype(BF16)
            y2_lo = (y2 - y2_hi.astype(F32)).astype(BF16)
            ss = (jnp.dot(y2_hi, ones_blk, preferred_element_type=F32)
                  + jnp.dot(y2_lo, ones_blk, preferred_element_type=F32))
            out_ref[rows, cols] = (y * lax.rsqrt(ss * (1.0 / HEAD_DIM) + EPS)
                                   * nw_ref[:, cols]).astype(BF16)


def _mlstm(proj, kt, g_row, rt, nw, B, S):
    T = B * S
    ts = TS_SEQ
    nst = S // ts

    def col(j):
        return pl.BlockSpec((ts, M_WIDTH), lambda b, s, j=j: (b * nst + s, j))

    return pl.pallas_call(
        _mlstm_kernel,
        grid=(B, nst),
        in_specs=[col(0), col(1), col(2),
                  pl.BlockSpec((M_WIDTH, ts), lambda b, s: (0, b * nst + s)),
                  pl.BlockSpec((SUBLANES, ts), lambda b, s: (0, b * nst + s)),
                  pl.BlockSpec((ts, LANES), lambda b, s: (b * nst + s, 0)),
                  pl.BlockSpec((1, M_WIDTH), lambda b, s: (0, 0))],
        out_specs=pl.BlockSpec((ts, M_WIDTH), lambda b, s: (b * nst + s, 0)),
        out_shape=jax.ShapeDtypeStruct((T, M_WIDTH), BF16),
        scratch_shapes=[
            pltpu.VMEM((M_HEADS, HEAD_DIM, 2 * HEAD_DIM), F32),
            pltpu.VMEM((SUBLANES, LANES), F32),
            pltpu.VMEM((3, M_HEADS, ts, LANES), F32),
        ],
        compiler_params=_cparams(("arbitrary", "arbitrary")),
        name="mlstm",
    )(proj, proj, proj, kt, g_row, rt, nw)


def _gelu_tanh(x):
    return 0.5 * x * (1.0 + jnp.tanh(0.7978845608028654 * (x + 0.044715 * (x * x * x))))


def _rglru_kernel(rx_ref, rg_ref, cw_ref, cb_ref, wri_ref, bri_ref, lam_ref, nw_ref, out_ref,
                  xpad_ref, a_ref, b_ref, hcar_ref):
    ts = rx_ref.shape[0]
    ng = ts // SUBLANES

    @pl.when(pl.program_id(1) == 0)
    def _():
        xpad_ref[0:SUBLANES, :] = jnp.zeros((SUBLANES, R_WIDTH), F32)
        hcar_ref[...] = jnp.zeros_like(hcar_ref)

    x = rx_ref[...].astype(F32)
    xpad_ref[SUBLANES:SUBLANES + ts, :] = x
    xp = xpad_ref[...]
    xc = cb_ref[...]
    for j in range(CONV_WIDTH):
        sh = CONV_WIDTH - 1 - j
        tap = xp if sh == 0 else pltpu.roll(xp, sh, axis=0)
        xc = xc + cw_ref[j:j + 1, :] * tap[SUBLANES:SUBLANES + ts, :]
    xpad_ref[0:SUBLANES, :] = x[ts - SUBLANES:ts, :]

    nlam = -lam_ref[...]
    softplus_nlam = jnp.maximum(nlam, 0.0) + jnp.log1p(jnp.exp(-jnp.abs(nlam)))
    row_in_group = lax.broadcasted_iota(I32, (ng, SUBLANES, R_BLOCK), 1)

    for h in range(R_HEADS):
        cols = slice(h * R_BLOCK, (h + 1) * R_BLOCK)
        xh = xc[:, cols]
        half_z = jnp.dot(xh.astype(BF16), wri_ref[h], preferred_element_type=F32) + bri_ref[h]
        t_r = jnp.tanh(half_z[:, :R_BLOCK])
        t_i = jnp.tanh(half_z[:, R_BLOCK:])
        c_half = (-0.5 * LRU_C) * softplus_nlam[:, cols]
        log_a = c_half * t_r + c_half
        a = jnp.exp(log_a)
        one_m_a2 = -jnp.tanh(log_a) * (a * a + 1.0)
        half_x = 0.5 * xh
        bt = jnp.sqrt(one_m_a2) * (half_x * t_i + half_x)
        a = a.reshape(ng, SUBLANES, R_BLOCK)
        bt = bt.reshape(ng, SUBLANES, R_BLOCK)
        d = 1
        while d < SUBLANES:
            keep = row_in_group >= d
            a_sh = pltpu.roll(a, d, axis=1)
            b_sh = pltpu.roll(bt, d, axis=1)
            bt = bt + jnp.where(keep, a * b_sh, 0.0)
            a = a * jnp.where(keep, a_sh, 1.0)
            d *= 2
        a_ref[:, cols] = a.reshape(ts, R_BLOCK)
        b_ref[:, cols] = bt.reshape(ts, R_BLOCK)

    def group_step(g, hprev):
        r0 = pl.multiple_of(g * SUBLANES, SUBLANES)
        hr = b_ref[pl.ds(r0, SUBLANES), :] + a_ref[pl.ds(r0, SUBLANES), :] * hprev
        b_ref[pl.ds(r0, SUBLANES), :] = hr
        return hr[SUBLANES - 1:SUBLANES, :]

    hcar_ref[...] = lax.fori_loop(0, ng, group_step, hcar_ref[...], unroll=4)

    y = _gelu_tanh(rg_ref[...].astype(F32)) * b_ref[...]
    out_ref[...] = _rms(y, nw_ref[...]).astype(BF16)


def _rglru(proj, cw, cb, wri, bri, lam, nw, B, S):
    T = B * S
    ts = TS_SEQ
    nst = S // ts

    def col(j):
        return pl.BlockSpec((ts, R_WIDTH), lambda b, s, j=j: (b * nst + s, j))

    def const(shape):
        return pl.BlockSpec(shape, lambda b, s: (0,) * len(shape))

    return pl.pallas_call(
        _rglru_kernel,
        grid=(B, nst),
        in_specs=[col(3), col(4), const((CONV_WIDTH, R_WIDTH)), const((1, R_WIDTH)),
                  const((R_HEADS, R_BLOCK, 2 * R_BLOCK)), const((R_HEADS, 1, 2 * R_BLOCK)),
                  const((1, R_WIDTH)), const((1, R_WIDTH))],
        out_specs=pl.BlockSpec((ts, R_WIDTH), lambda b, s: (b * nst + s, 0)),
        out_shape=jax.ShapeDtypeStruct((T, R_WIDTH), BF16),
        scratch_shapes=[
            pltpu.VMEM((ts + SUBLANES, R_WIDTH), F32),
            pltpu.VMEM((ts, R_WIDTH), F32),
            pltpu.VMEM((ts, R_WIDTH), F32),
            pltpu.VMEM((1, R_WIDTH), F32),
        ],
        compiler_params=_cparams(("arbitrary", "arbitrary")),
        name="rglru",
    )(proj, proj, cw, cb, wri, bri, lam, nw)


def _outproj_kernel(hm_ref, hr_ref, x_ref, wo_ref, n2_ref, rw2_ref, rb_ref,
                    h1_ref, xn_ref, lg_ref):
    mix = jnp.dot(hm_ref[...], wo_ref[0:M_WIDTH, :], preferred_element_type=F32)
    mix = mix + jnp.dot(hr_ref[...], wo_ref[M_WIDTH:, :], preferred_element_type=F32)
    h1 = x_ref[...] + mix
    h1_ref[...] = h1
    xn = _rms(h1, n2_ref[...])
    xn_ref[...] = xn
    xh = xn.astype(BF16)
    xl = (xn - xh.astype(F32)).astype(BF16)
    lg2 = jnp.dot(xh, rw2_ref[...], preferred_element_type=F32)
    lg = lg2[:, :LANES] + lg2[:, LANES:]
    lg = lg + jnp.dot(xl, rw2_ref[:, :LANES], preferred_element_type=F32)
    lg_ref[...] = lg.T[0:N_EXPERTS] + rb_ref[...]


def _outproj(hm, hr, x2, wo, n2, rw2, rb):
    T = x2.shape[0]
    tm = TM_PROJ

    def const(shape):
        return pl.BlockSpec(shape, lambda i: (0,) * len(shape))

    return pl.pallas_call(
        _outproj_kernel,
        grid=(T // tm,),
        in_specs=[
            pl.BlockSpec((tm, M_WIDTH), lambda i: (i, 0)),
            pl.BlockSpec((tm, R_WIDTH), lambda i: (i, 0)),
            pl.BlockSpec((tm, D_MODEL), lambda i: (i, 0)),
            const((D_MODEL, D_MODEL)), const((1, D_MODEL)),
            const((D_MODEL, 2 * LANES)), const((N_EXPERTS, 1)),
        ],
        out_specs=[
            pl.BlockSpec((tm, D_MODEL), lambda i: (i, 0)),
            pl.BlockSpec((tm, D_MODEL), lambda i: (i, 0)),
            pl.BlockSpec((N_EXPERTS, tm), lambda i: (0, i)),
        ],
        out_shape=[
            jax.ShapeDtypeStruct((T, D_MODEL), F32),
            jax.ShapeDtypeStruct((T, D_MODEL), F32),
            jax.ShapeDtypeStruct((N_EXPERTS, T), F32),
        ],
        compiler_params=_cparams(("arbitrary",)),
        name="outproj",
    )(hm, hr, x2, wo, n2, rw2, rb)


def _route_kernel(lg_ref, ids_ref, rank_ref, meta_ref, cnt_ref, carry_ref):
    tn = lg_ref.shape[1]
    sub = 256

    @pl.when(pl.program_id(0) == 0)
    def _():
        carry_ref[...] = jnp.zeros_like(carry_ref)

    logits = lg_ref[...]
    e_idx = lax.broadcasted_iota(I32, (N_EXPERTS, tn), 0)
    vals, idxs, sels = [], [], []
    for _ in range(TOP_K):
        m = jnp.max(logits, axis=0, keepdims=True)
        idx = jnp.min(jnp.where(logits == m, e_idx, N_EXPERTS), axis=0, keepdims=True)
        sel = e_idx == idx
        logits = jnp.where(sel, -jnp.inf, logits)
        vals.append(m)
        idxs.append(idx)
        sels.append(sel)
    exps = [jnp.exp(v - vals[0]) for v in vals]
    denom = exps[0] + exps[1] + exps[2] + exps[3]
    chosen = (sels[0] | sels[1] | sels[2] | sels[3]).astype(BF16)

    s_i = lax.broadcasted_iota(I32, (sub, sub), 0)
    t_i = lax.broadcasted_iota(I32, (sub, sub), 1)
    upper = (s_i < t_i).astype(BF16)
    carry = carry_ref[:, 0:1]
    parts = []
    for c in range(tn // sub):
        mc = chosen[:, c * sub:(c + 1) * sub]
        cs = jnp.dot(mc, upper, preferred_element_type=F32) + carry
        parts.append(cs)
        carry = cs[:, sub - 1:sub] + mc[:, sub - 1:sub].astype(F32)
    before = jnp.concatenate(parts, axis=1)
    carry_ref[...] = jnp.broadcast_to(carry, carry_ref.shape)
    cnt_ref[...] = jnp.broadcast_to(carry, cnt_ref.shape)

    zeros_i = jnp.zeros((SUBLANES - TOP_K, tn), I32)
    ranks = [jnp.sum(jnp.where(s, before, 0.0), axis=0, keepdims=True).astype(I32) for s in sels]
    ids_ref[...] = jnp.concatenate(idxs + [zeros_i], axis=0)
    rank_ref[...] = jnp.concatenate(ranks + [zeros_i], axis=0)
    tok = (pl.program_id(0) * tn + lax.broadcasted_iota(I32, (1, tn), 1)).astype(F32)
    meta_ref[...] = jnp.concatenate(
        [tok] + [i.astype(F32) for i in idxs] + [e / denom for e in exps]
        + [jnp.ones((1, tn), F32), jnp.zeros((META_ROWS - 2 - 2 * TOP_K, tn), F32)], axis=0)


def _route(logits_t):
    T = logits_t.shape[1]
    tn = TN_ROUTE
    tok = pl.BlockSpec((SUBLANES, tn), lambda i: (0, i))
    return pl.pallas_call(
        _route_kernel,
        grid=(T // tn,),
        in_specs=[pl.BlockSpec((N_EXPERTS, tn), lambda i: (0, i))],
        out_specs=[tok, tok, pl.BlockSpec((META_ROWS, tn), lambda i: (0, i)),
                   pl.BlockSpec((N_EXPERTS, LANES), lambda i: (0, 0))],
        out_shape=[
            jax.ShapeDtypeStruct((SUBLANES, T), I32),
            jax.ShapeDtypeStruct((SUBLANES, T), I32),
            jax.ShapeDtypeStruct((META_ROWS, T), F32),
            jax.ShapeDtypeStruct((N_EXPERTS, LANES), F32),
        ],
        scratch_shapes=[pltpu.VMEM((N_EXPERTS, LANES), F32)],
        compiler_params=_cparams(("arbitrary",)),
        name="route",
    )(logits_t)


def _dest_kernel(pst_ref, ids_ref, rank_ref, dest_ref):
    ids = ids_ref[...]
    start = jnp.zeros(ids.shape, I32)
    for e in range(N_EXPERTS):
        start = jnp.where(ids == e, pst_ref[e], start)
    dest = rank_ref[...] + start
    for j in range(dest_ref.shape[0]):
        dest_ref[j] = dest[:, j * TT_ROWS:(j + 1) * TT_ROWS]


def _dest(pstarts, ids, rank):
    T = ids.shape[1]
    tn = TN_ROUTE
    per = tn // TT_ROWS
    tok = pl.BlockSpec((SUBLANES, tn), lambda i, pst: (0, i))
    return pl.pallas_call(
        _dest_kernel,
        grid_spec=pltpu.PrefetchScalarGridSpec(
            num_scalar_prefetch=1,
            grid=(T // tn,),
            in_specs=[tok, tok],
            out_specs=pl.BlockSpec((per, SUBLANES, TT_ROWS), lambda i, pst: (i, 0, 0)),
        ),
        out_shape=jax.ShapeDtypeStruct((T // TT_ROWS, SUBLANES, TT_ROWS), I32),
        compiler_params=_cparams(("arbitrary",)),
        name="dest",
    )(pstarts, ids, rank)


def _rows_to_tiles(cols):
    return pltpu.einshape("jtl->tjl", jnp.stack(cols, axis=0))


def _tiles_to_rows(tiles):
    return pltpu.einshape("tjl->jtl", tiles)


def _dispatch_kernel(cnt_ref, pst_ref, pad_ref, nb_ref, dest_hbm, x_ref, meta_ref, xs_hbm,
                     dsm, ext, zbuf, sem_d, sem_x, sem_z):
    tt = x_ref.shape[0]
    i = pl.program_id(0)
    n = pl.num_programs(0)
    slot = i % 2
    n_blocks = xs_hbm.shape[0] // TM_MOE

    def dest_copy(j, sl):
        return pltpu.make_async_copy(dest_hbm.at[j], dsm.at[sl], sem_d.at[sl])

    def zero_copy(r):
        return pltpu.make_async_copy(zbuf.at[0], xs_hbm.at[r], sem_z)

    def zero_block_copy(b):
        r0 = pl.multiple_of(b * TM_MOE, TM_MOE)
        return pltpu.make_async_copy(zbuf, xs_hbm.at[pl.ds(r0, TM_MOE)], sem_z)

    def rows_wait(sl):
        for _ in range(TOP_K):
            pltpu.make_async_copy(ext.at[sl], xs_hbm.at[pl.ds(0, tt)], sem_x.at[sl]).wait()

    def for_each_pad_row(fn):
        def per_expert(e, carry):
            lo = pst_ref[e] + cnt_ref[e]
            hi = pst_ref[e] + pad_ref[e]

            def per_row(r, c2):
                fn(r)
                return c2

            return lax.fori_loop(lo, hi, per_row, carry)

        lax.fori_loop(0, N_EXPERTS, per_expert, 0)

    def for_each_unused_block(fn):
        def per_block(b, carry):
            fn(b)
            return carry

        lax.fori_loop(nb_ref[0], n_blocks, per_block, 0)

    @pl.when(i == 0)
    def _():
        dest_copy(0, 0).start()
        zbuf[...] = jnp.zeros_like(zbuf)
        for_each_pad_row(lambda r: zero_copy(r).start())
        for_each_unused_block(lambda b: zero_block_copy(b).start())

    xbits = pltpu.bitcast(x_ref[...].astype(BF16).astype(F32), U32)
    cols = []
    for s in range(D_MODEL // (2 * LANES)):
        hi = xbits[:, (2 * s) * LANES:(2 * s + 1) * LANES]
        lo = xbits[:, (2 * s + 1) * LANES:(2 * s + 2) * LANES]
        cols.append(hi | (lo >> 16))
    meta_pad = jnp.concatenate([meta_ref[...], jnp.zeros((LANES - META_ROWS, tt), F32)], axis=0)
    cols.append(pltpu.bitcast(meta_pad.T, U32))
    cols += [jnp.zeros((tt, LANES), U32)] * (SUBLANES - len(cols))
    ext[slot] = _rows_to_tiles(cols)

    dest_copy(i, slot).wait()

    @pl.when(i + 1 < n)
    def _():
        dest_copy(i + 1, 1 - slot).start()

    group = 16
    for t0 in range(0, tt, group):
        dsts = [[dsm[slot, k, t] for k in range(TOP_K)] for t in range(t0, t0 + group)]
        for j, t in enumerate(range(t0, t0 + group)):
            for k in range(TOP_K):
                pltpu.make_async_copy(ext.at[slot, t], xs_hbm.at[dsts[j][k]],
                                      sem_x.at[slot]).start(priority=k % 2)

    @pl.when(i == 0)
    def _():
        for_each_pad_row(lambda r: zero_copy(r).wait())
        for_each_unused_block(lambda b: zero_block_copy(b).wait())

    @pl.when(i > 0)
    def _():
        rows_wait(1 - slot)

    @pl.when(i == n - 1)
    def _():
        rows_wait(slot)


def _dispatch(counts, pstarts, padded, nb_used, dest_tiles, xn, meta, n_rows):
    T = xn.shape[0]
    tt = TT_ROWS
    return pl.pallas_call(
        _dispatch_kernel,
        grid_spec=pltpu.PrefetchScalarGridSpec(
            num_scalar_prefetch=4,
            grid=(T // tt,),
            in_specs=[
                pl.BlockSpec(memory_space=pl.ANY),
                pl.BlockSpec((tt, D_MODEL), lambda i, *_: (i, 0)),
                pl.BlockSpec((META_ROWS, tt), lambda i, *_: (0, i)),
            ],
            out_specs=pl.BlockSpec(memory_space=pl.ANY),
            scratch_shapes=[
                pltpu.SMEM((2, SUBLANES, tt), I32),
                pltpu.VMEM((2, tt, SUBLANES, LANES), U32),
                pltpu.VMEM((TM_MOE, SUBLANES, LANES), U32),
                pltpu.SemaphoreType.DMA((2,)),
                pltpu.SemaphoreType.DMA((2,)),
                pltpu.SemaphoreType.DMA,
            ],
        ),
        out_shape=jax.ShapeDtypeStruct((n_rows, SUBLANES, LANES), U32),
        compiler_params=_cparams(("arbitrary",)),
        name="dispatch",
    )(counts, pstarts, padded, nb_used, dest_tiles, xn, meta)


def _experts_kernel(be_ref, nb_ref, xs_ref, wgu_ref, bgu_ref, wd_ref, bd_ref, out_hbm,
                    wgu_bf, wd_bf, acc_ref, ybuf, code_v, code_s, sem_c, sem_y):
    tm = TM_MOE
    b = pl.program_id(0)
    n = pl.num_programs(0)
    slot = b % 2
    nbu = nb_ref[0]
    n_tok = (out_hbm.shape[0] - 2 * tm) // TOP_K
    dump0 = TOP_K * n_tok

    def scatter_wait(sl):
        pltpu.make_async_copy(ybuf.at[sl], out_hbm.at[pl.ds(0, tm)], sem_y.at[sl]).wait()

    def scatter_row(sl, r, dst):
        return pltpu.make_async_copy(ybuf.at[sl, r], out_hbm.at[dst], sem_y.at[sl])

    def code_copy(sl):
        return pltpu.make_async_copy(code_v, code_s.at[sl], sem_c)

    @pl.when(b == 0)
    def _():
        ybuf[...] = jnp.zeros_like(ybuf)
        for sl in range(2):
            cp = pltpu.make_async_copy(ybuf.at[sl], out_hbm.at[pl.ds(dump0 + sl * tm, tm)],
                                       sem_y.at[sl])
            cp.start()
            cp.wait()

        def init_code(r, carry):
            code_s[1, 0, r] = dump0 + tm + r
            return carry

        lax.fori_loop(0, tm, init_code, 0)

    @pl.when(b < nbu)
    def _():
        e = be_ref[b]
        prev = be_ref[jnp.maximum(b - 1, 0)]

        @pl.when((b == 0) | (e != prev))
        def _():
            wgu_bf[...] = wgu_ref[0].astype(BF16)
            wd_bf[...] = wd_ref[0].astype(BF16)

        planes = _tiles_to_rows(xs_ref[...])
        meta = pltpu.bitcast(planes[D_MODEL // (2 * LANES)], F32)
        e_f = e.astype(F32)
        tok = meta[:, 0:1]
        valid = meta[:, 1 + 2 * TOP_K:2 + 2 * TOP_K] > 0.5
        choice = jnp.zeros((tm, 1), F32)
        gate = jnp.zeros((tm, 1), F32)
        for j in range(TOP_K):
            hit = meta[:, 1 + j:2 + j] == e_f
            choice = jnp.where(hit, float(j), choice)
            gate = jnp.where(hit, meta[:, 1 + TOP_K + j:2 + TOP_K + j], gate)
        spare = (dump0 + slot * tm + lax.broadcasted_iota(I32, (tm, 1), 0)).astype(F32)
        code = jnp.where(valid, choice * float(n_tok) + tok, spare)
        code_row = jnp.broadcast_to(code, (tm, LANES)).T[0:1, :]
        code_v[...] = code_row.astype(I32)
        code_copy(slot).start()

        xparts = []
        for s in range(D_MODEL // (2 * LANES)):
            w = planes[s]
            xparts.append(pltpu.bitcast(w & jnp.uint32(0xFFFF0000), F32).astype(BF16))
            xparts.append(pltpu.bitcast(w << 16, F32).astype(BF16))
        x = jnp.concatenate(xparts, axis=1)

        group = 16
        for r0 in range(0, tm, group):
            dsts = [code_s[1 - slot, 0, r] for r in range(r0, r0 + group)]
            for j, r in enumerate(range(r0, r0 + group)):
                scatter_row(1 - slot, r, dsts[j]).start(priority=r % 2)

        for c in range(D_FF // FF_CHUNK):
            gcols = slice(c * FF_CHUNK, (c + 1) * FF_CHUNK)
            ucols = slice(D_FF + c * FF_CHUNK, D_FF + (c + 1) * FF_CHUNK)
            g = jnp.dot(x, wgu_bf[:, gcols], preferred_element_type=F32) + bgu_ref[0, :, gcols]
            up = jnp.dot(x, wgu_bf[:, ucols], preferred_element_type=F32) + bgu_ref[0, :, ucols]
            g = jnp.minimum(g, SWIGLU_LIMIT)
            up = jnp.clip(up, -SWIGLU_LIMIT, SWIGLU_LIMIT)
            hdn = ((up + 1.0) * g * jax.nn.sigmoid(g * SWIGLU_ALPHA)).astype(BF16)
            part = jnp.dot(hdn, wd_bf[gcols, :], preferred_element_type=F32)
            if c == 0:
                acc_ref[...] = part
            else:
                acc_ref[...] += part

        @pl.when(b > 0)
        def _():
            scatter_wait(slot)

        y = (acc_ref[...] + bd_ref[0]) * gate
        ybuf[slot] = _rows_to_tiles([y[:, j * LANES:(j + 1) * LANES]
                                     for j in range(D_MODEL // LANES)])
        code_copy(slot).wait()

    @pl.when(b == nbu)
    def _():
        scatter_wait(slot)

        def per_row(r, carry):
            scatter_row(1 - slot, r, code_s[1 - slot, 0, r]).start()
            return carry

        lax.fori_loop(0, tm, per_row, 0, unroll=8)
        scatter_wait(1 - slot)


def _experts(block_e, nb_used, xs, wgu, bgu, wd, bd, n_tok):
    P = xs.shape[0]
    tm = TM_MOE
    nb = P // tm

    def row_block(b, be, nbu):
        return (jnp.minimum(b, nbu[0] - 1), 0, 0)

    def expert_block(b, be, nbu):
        return (be[jnp.minimum(b, nbu[0] - 1)], 0, 0)

    return pl.pallas_call(
        _experts_kernel,
        grid_spec=pltpu.PrefetchScalarGridSpec(
            num_scalar_prefetch=2,
            grid=(nb,),
            in_specs=[
                pl.BlockSpec((tm, SUBLANES, LANES), row_block),
                pl.BlockSpec((1, D_MODEL, 2 * D_FF), expert_block),
                pl.BlockSpec((1, 1, 2 * D_FF), expert_block),
                pl.BlockSpec((1, D_FF, D_MODEL), expert_block),
                pl.BlockSpec((1, 1, D_MODEL), expert_block),
            ],
            out_specs=pl.BlockSpec(memory_space=pl.ANY),
            scratch_shapes=[
                pltpu.VMEM((D_MODEL, 2 * D_FF), BF16),
                pltpu.VMEM((D_FF, D_MODEL), BF16),
                pltpu.VMEM((tm, D_MODEL), F32),
                pltpu.VMEM((2, tm, SUBLANES, LANES), F32),
                pltpu.VMEM((1, tm), I32),
                pltpu.SMEM((2, 1, tm), I32),
                pltpu.SemaphoreType.DMA,
                pltpu.SemaphoreType.DMA((2,)),
            ],
        ),
        out_shape=jax.ShapeDtypeStruct((TOP_K * n_tok + 2 * tm, SUBLANES, LANES), F32),
        compiler_params=_cparams(("arbitrary",)),
        name="experts",
    )(block_e, nb_used, xs, wgu, bgu, wd, bd)


def _combine_kernel(h1_ref, p0_ref, p1_ref, p2_ref, p3_ref, fw_ref, out_ref):
    moe_tiles = p0_ref[...] + p1_ref[...] + p2_ref[...] + p3_ref[...]
    moe_cols = _tiles_to_rows(moe_tiles)
    moe = jnp.concatenate([moe_cols[j] for j in range(D_MODEL // LANES)], axis=1)
    out_ref[...] = _rms(h1_ref[...] + moe, fw_ref[...])


def _combine(h1, planes, fw):
    T = h1.shape[0]
    tt = TT_COMBINE
    nt = T // tt

    def plane(k):
        return pl.BlockSpec((tt, SUBLANES, LANES), lambda i, k=k: (k * nt + i, 0, 0))

    return pl.pallas_call(
        _combine_kernel,
        grid=(nt,),
        in_specs=[pl.BlockSpec((tt, D_MODEL), lambda i: (i, 0)),
                  plane(0), plane(1), plane(2), plane(3),
                  pl.BlockSpec((1, D_MODEL), lambda i: (0, 0))],
        out_specs=pl.BlockSpec((tt, D_MODEL), lambda i: (i, 0)),
        out_shape=jax.ShapeDtypeStruct((T, D_MODEL), F32),
        compiler_params=_cparams(("arbitrary",)),
        name="combine",
    )(h1, planes, planes, planes, planes, fw)


def kernel(x, norm1_w, w_in, mlstm_gate_b, mlstm_norm_w, conv_w, conv_b, lru_wr, lru_br, lru_wi,
           lru_bi, lru_lambda, lru_norm_w, w_out, norm2_w, router_w, router_b, moe_w_gu, moe_b_gu,
           moe_w_down, moe_b_down, final_norm_w):
    B, S, D = x.shape
    assert D == D_MODEL and norm1_w.shape[0] == 1
    assert S % TS_SEQ == 0 and (B * S) % TN_ROUTE == 0
    T = B * S
    x2 = x.reshape(T, D)

    w = w_in[0]
    k_off = M_WIDTH
    v_off = 2 * M_WIDTH
    g_off = 3 * M_WIDTH
    o_off = g_off + 2 * M_HEADS
    w_main = jnp.concatenate([w[:, :k_off], w[:, v_off:g_off], w[:, o_off:]], axis=1).astype(BF16)
    w_kt = w[:, k_off:v_off].T.astype(BF16)
    w_gate = jnp.zeros((D, LANES), F32)
    w_gate = w_gate.at[:, 0:M_HEADS].set(w[:, g_off:g_off + M_HEADS])
    w_gate = w_gate.at[:, SUBLANES:SUBLANES + M_HEADS].set(w[:, g_off + M_HEADS:o_off])
    gate_b = jnp.zeros((LANES, 1), F32)
    gate_b = gate_b.at[0:M_HEADS, 0].set(mlstm_gate_b[0, :M_HEADS])
    gate_b = gate_b.at[SUBLANES:SUBLANES + M_HEADS, 0].set(mlstm_gate_b[0, M_HEADS:])
    wri = (0.5 * jnp.concatenate([lru_wr[0], lru_wi[0]], axis=-1)).astype(BF16)
    bri = 0.5 * jnp.concatenate([lru_br[0].reshape(R_HEADS, 1, R_BLOCK),
                                 lru_bi[0].reshape(R_HEADS, 1, R_BLOCK)], axis=-1)
    rw = jnp.zeros((D, LANES), F32).at[:, :N_EXPERTS].set(router_w[0])
    rwh = rw.astype(BF16)
    rwl = (rw - rwh.astype(F32)).astype(BF16)
    rw2 = jnp.concatenate([rwh, rwl], axis=1)

    proj, kt, g_row, rt = _inproj(x2, norm1_w, w_main, w_kt, w_gate.astype(BF16), gate_b)
    h_m = _mlstm(proj, kt, g_row, rt, mlstm_norm_w, B, S)
    h_r = _rglru(proj, conv_w[0], conv_b, wri, bri, lru_lambda, lru_norm_w, B, S)
    h1, xn2, logits_t = _outproj(h_m, h_r, x2, w_out[0].astype(BF16), norm2_w, rw2,
                                 router_b[0].reshape(N_EXPERTS, 1))

    ids, rank, meta, cnt = _route(logits_t)
    counts = cnt[:, 0].astype(I32)
    padded = ((counts + TM_MOE - 1) // TM_MOE) * TM_MOE
    pends = jnp.cumsum(padded)
    pstarts = pends - padded
    nb = (T * TOP_K) // TM_MOE + N_EXPERTS
    nb_used = (pends[-1:] // TM_MOE).astype(I32)
    block_start = jnp.arange(nb, dtype=I32) * TM_MOE
    block_e = jnp.minimum(jnp.sum((pends[None, :] <= block_start[:, None]).astype(I32), axis=1),
                          N_EXPERTS - 1)

    dest_tiles = _dest(pstarts, ids, rank)
    xs = _dispatch(counts, pstarts, padded, nb_used, dest_tiles, xn2, meta, nb * TM_MOE)
    planes = _experts(block_e, nb_used, xs, moe_w_gu[0],
                      moe_b_gu[0].reshape(N_EXPERTS, 1, 2 * D_FF),
                      moe_w_down[0], moe_b_down[0].reshape(N_EXPERTS, 1, D), T)
    out = _combine(h1, planes, final_norm_w.reshape(1, D))
    return out.reshape(B, S, D)
```

```python
import jax
import jax.numpy as jnp
from jax import lax
from jax.experimental import pallas as pl
from jax.experimental.pallas import tpu as pltpu

F32 = jnp.float32
BF16 = jnp.bfloat16
I32 = jnp.int32

D_MODEL = 1024
M_HEADS = 4
HEAD_DIM = 128
M_WIDTH = M_HEADS * HEAD_DIM
R_HEADS = 4
R_BLOCK = 128
R_WIDTH = R_HEADS * R_BLOCK
CONV_WIDTH = 4
LRU_C = 8.0
N_EXPERTS = 32
TOP_K = 4
D_FF = 1024
SWIGLU_LIMIT = 7.0
SWIGLU_ALPHA = 1.702
EPS = 1e-6

N_MAIN = 5 * 512
LANES = 128
SUBLANES = 8
VMEM_LIMIT = 56 * 1024 * 1024

TM_PROJ = 512
TS_SEQ = 512
CHUNK = 128
TN_ROUTE = 2048
TT_ROWS = 256
TT_COMBINE = 512
TM_MOE = 512
FF_CHUNK = 512
META_ROWS = 16
ROW_SUBLANES = 16
NEG_BIG = -1e30


def _cparams(sem, vmem=VMEM_LIMIT):
    return pltpu.CompilerParams(dimension_semantics=sem, vmem_limit_bytes=vmem)


def _rms(x, w):
    ms = jnp.mean(x * x, axis=-1, keepdims=True)
    return x * lax.rsqrt(ms + EPS) * w


def _sigmoid(x):
    return 0.5 * jnp.tanh(0.5 * x) + 0.5


def _segmented_scan(x, op, ident, lane_in_chunk):
    d = 1
    while d < CHUNK:
        shifted = pltpu.roll(x, d, axis=1)
        x = op(x, jnp.where(lane_in_chunk >= d, shifted, ident))
        d *= 2
    return x


def _inproj_kernel(x_ref, nw_ref, w_ref, wkt_ref, wg_ref, gb_ref,
                   proj_ref, kt_ref, grow_ref, rt_ref):
    tm = x_ref.shape[0]
    xn = _rms(x_ref[...], nw_ref[...]).astype(BF16)

    g = jnp.dot(xn, wg_ref[...], preferred_element_type=F32)
    gt = g.T + gb_ref[...]
    ig = gt[0:SUBLANES]
    fg = gt[SUBLANES:2 * SUBLANES]
    lf = jnp.minimum(fg, 0.0) - jnp.log1p(jnp.exp(-jnp.abs(fg)))
    lane_in_chunk = lax.broadcasted_iota(I32, (SUBLANES, tm), 1) % CHUNK
    b_row = _segmented_scan(lf, jnp.add, 0.0, lane_in_chunk)
    g_row = ig - b_row
    cm_row = _segmented_scan(g_row, jnp.maximum, NEG_BIG, lane_in_chunk)
    grow_ref[...] = g_row
    stacked = jnp.concatenate(
        [b_row, g_row, cm_row, jnp.zeros((LANES - 3 * SUBLANES, tm), F32)], axis=0)
    rt_ref[...] = stacked.T

    for c in range(N_MAIN // 512):
        cols = slice(c * 512, (c + 1) * 512)
        proj_ref[:, cols] = jnp.dot(xn, w_ref[:, cols], preferred_element_type=F32).astype(BF16)

    kt_ref[...] = lax.dot_general(wkt_ref[...], xn, (((1,), (1,)), ((), ())),
                                  preferred_element_type=F32).astype(BF16)


def _inproj(x2, nw, w_main, w_kt, w_gate, gate_b):
    T = x2.shape[0]
    tm = TM_PROJ

    def const(shape):
        return pl.BlockSpec(shape, lambda i: (0,) * len(shape))

    return pl.pallas_call(
        _inproj_kernel,
        grid=(T // tm,),
        in_specs=[
            pl.BlockSpec((tm, D_MODEL), lambda i: (i, 0)),
            const((1, D_MODEL)), const((D_MODEL, N_MAIN)), const((M_WIDTH, D_MODEL)),
            const((D_MODEL, LANES)), const((LANES, 1)),
        ],
        out_specs=[
            pl.BlockSpec((tm, N_MAIN), lambda i: (i, 0)),
            pl.BlockSpec((M_WIDTH, tm), lambda i: (0, i)),
            pl.BlockSpec((SUBLANES, tm), lambda i: (0, i)),
            pl.BlockSpec((tm, LANES), lambda i: (i, 0)),
        ],
        out_shape=[
            jax.ShapeDtypeStruct((T, N_MAIN), BF16),
            jax.ShapeDtypeStruct((M_WIDTH, T), BF16),
            jax.ShapeDtypeStruct((SUBLANES, T), F32),
            jax.ShapeDtypeStruct((T, LANES), F32),
        ],
        compiler_params=_cparams(("arbitrary",)),
        name="inproj",
    )(x2, nw, w_main, w_kt, w_gate, gate_b)


def _mlstm_kernel(q_ref, v_ref, o_ref, kt_ref, grow_ref, rt_ref, nw_ref, out_ref,
                  c_ref, m_ref, bc_ref):
    ts = q_ref.shape[0]
    scale = HEAD_DIM ** -0.5

    @pl.when(pl.program_id(1) == 0)
    def _():
        c_ref[...] = jnp.zeros_like(c_ref)
        m_ref[...] = jnp.zeros_like(m_ref)

    for term in range(3):
        for h in range(M_HEADS):
            col = term * SUBLANES + h
            bc_ref[term, h] = jnp.broadcast_to(rt_ref[:, col:col + 1], (ts, LANES))

    t_idx = lax.broadcasted_iota(I32, (CHUNK, CHUNK), 0)
    s_idx = lax.broadcasted_iota(I32, (CHUNK, CHUNK), 1)
    causal = t_idx >= s_idx
    ones_blk = jnp.ones((CHUNK, HEAD_DIM), BF16)

    for c in range(ts // CHUNK):
        rows = slice(c * CHUNK, (c + 1) * CHUNK)
        last = slice((c + 1) * CHUNK - 1, (c + 1) * CHUNK)
        for h in range(M_HEADS):
            cols = slice(h * HEAD_DIM, (h + 1) * HEAD_DIM)
            qc = q_ref[rows, cols]
            kt = kt_ref[cols, rows]
            v_ext = jnp.concatenate([v_ref[rows, cols], ones_blk], axis=1)
            b_bc = bc_ref[0, h, rows, :]
            cm_bc = bc_ref[2, h, rows, :]
            g_r = grow_ref[h:h + 1, rows]
            m_prev = m_ref[h:h + 1, :]
            c_prev = c_ref[h]

            mx = jnp.maximum(cm_bc, m_prev)
            decay_mat = jnp.exp(jnp.where(causal, g_r - mx, NEG_BIG))
            scores = jnp.dot(qc, kt, preferred_element_type=F32)
            p = (scores * decay_mat * scale).astype(BF16)
            intra = jnp.dot(p, v_ext, preferred_element_type=F32)
            inter = jnp.dot(qc, c_prev.astype(BF16), preferred_element_type=F32)
            w_inter = jnp.exp(m_prev - mx)
            num = intra[:, :HEAD_DIM] + w_inter * inter[:, :HEAD_DIM]
            den = intra[:, HEAD_DIM:] + w_inter * inter[:, HEAD_DIM:]
            m_t = b_bc + mx
            h_t = num / jnp.maximum(jnp.abs(den), jnp.exp(-m_t))

            mx_last = jnp.maximum(bc_ref[2, h, last, :], m_prev)
            w_row = jnp.exp(g_r - mx_last) * scale
            kw_t = (kt.astype(F32) * w_row).astype(BF16)
            upd = jnp.dot(kw_t, v_ext, preferred_element_type=F32)
            dec = jnp.exp(m_prev - mx_last)
            c_ref[h] = jnp.concatenate([dec, dec], axis=1) * c_prev + upd
            m_ref[h:h + 1, :] = bc_ref[0, h, last, :] + mx_last

            y = _sigmoid(o_ref[rows, cols].astype(F32)) * h_t
            y2 = y * y
            y2_hi = y2.astype(BF16)
            y2_lo = (y2 - y2_hi.astype(F32)).astype(BF16)
            ss = (jnp.dot(y2_hi, ones_blk, preferred_element_type=F32)
                  + jnp.dot(y2_lo, ones_blk, preferred_element_type=F32))
            out_ref[rows, cols] = (y * lax.rsqrt(ss * (1.0 / HEAD_DIM) + EPS)
                                   * nw_ref[:, cols]).astype(BF16)


def _mlstm(proj, kt, g_row, rt, nw, B, S):
    T = B * S
    ts = TS_SEQ
    nst = S // ts

    def col(j):
        return pl.BlockSpec((ts, M_WIDTH), lambda b, s, j=j: (b * nst + s, j))

    return pl.pallas_call(
        _mlstm_kernel,
        grid=(B, nst),
        in_specs=[col(0), col(1), col(2),
                  pl.BlockSpec((M_WIDTH, ts), lambda b, s: (0, b * nst + s)),
                  pl.BlockSpec((SUBLANES, ts), lambda b, s: (0, b * nst + s)),
                  pl.BlockSpec((ts, LANES), lambda b, s: (b * nst + s, 0)),
                  pl.BlockSpec((1, M_WIDTH), lambda b, s: (0, 0))],
        out_specs=pl.BlockSpec((ts, M_WIDTH), lambda b, s: (b * nst + s, 0)),
        out_shape=jax.ShapeDtypeStruct((T, M_WIDTH), BF16),
        scratch_shapes=[
            pltpu.VMEM((M_HEADS, HEAD_DIM, 2 * HEAD_DIM), F32),
            pltpu.VMEM((SUBLANES, LANES), F32),
            pltpu.VMEM((3, M_HEADS, ts, LANES), F32),
        ],
        compiler_params=_cparams(("arbitrary", "arbitrary")),
        name="mlstm",
    )(proj, proj, proj, kt, g_row, rt, nw)


def _gelu_tanh(x):
    return 0.5 * x * (1.0 + jnp.tanh(0.7978845608028654 * (x + 0.044715 * (x * x * x))))


def _rglru_kernel(rx_ref, rg_ref, cw_ref, cb_ref, wri_ref, bri_ref, lam_ref, nw_ref, out_ref,
                  xpad_ref, a_ref, b_ref, hcar_ref):
    ts = rx_ref.shape[0]
    ng = ts // SUBLANES

    @pl.when(pl.program_id(1) == 0)
    def _():
        xpad_ref[0:SUBLANES, :] = jnp.zeros((SUBLANES, R_WIDTH), F32)
        hcar_ref[...] = jnp.zeros_like(hcar_ref)

    x = rx_ref[...].astype(F32)
    xpad_ref[SUBLANES:SUBLANES + ts, :] = x
    xp = xpad_ref[...]
    xc = cb_ref[...]
    for j in range(CONV_WIDTH):
        sh = CONV_WIDTH - 1 - j
        tap = xp if sh == 0 else pltpu.roll(xp, sh, axis=0)
        xc = xc + cw_ref[j:j + 1, :] * tap[SUBLANES:SUBLANES + ts, :]
    xpad_ref[0:SUBLANES, :] = x[ts - SUBLANES:ts, :]

    nlam = -lam_ref[...]
    softplus_nlam = jnp.maximum(nlam, 0.0) + jnp.log1p(jnp.exp(-jnp.abs(nlam)))
    row_in_group = lax.broadcasted_iota(I32, (ng, SUBLANES, R_BLOCK), 1)

    for h in range(R_HEADS):
        cols = slice(h * R_BLOCK, (h + 1) * R_BLOCK)
        xh = xc[:, cols]
        half_z = jnp.dot(xh.astype(BF16), wri_ref[h], preferred_element_type=F32) + bri_ref[h]
        t_r = jnp.tanh(half_z[:, :R_BLOCK])
        t_i = jnp.tanh(half_z[:, R_BLOCK:])
        c_half = (-0.5 * LRU_C) * softplus_nlam[:, cols]
        log_a = c_half * t_r + c_half
        a = jnp.exp(log_a)
        one_m_a2 = -jnp.tanh(log_a) * (a * a + 1.0)
        half_x = 0.5 * xh
        bt = jnp.sqrt(one_m_a2) * (half_x * t_i + half_x)
        a = a.reshape(ng, SUBLANES, R_BLOCK)
        bt = bt.reshape(ng, SUBLANES, R_BLOCK)
        d = 1
        while d < SUBLANES:
            keep = row_in_group >= d
            a_sh = pltpu.roll(a, d, axis=1)
            b_sh = pltpu.roll(bt, d, axis=1)
            bt = bt + jnp.where(keep, a * b_sh, 0.0)
            a = a * jnp.where(keep, a_sh, 1.0)
            d *= 2
        a_ref[:, cols] = a.reshape(ts, R_BLOCK)
        b_ref[:, cols] = bt.reshape(ts, R_BLOCK)

    def group_step(g, hprev):
        r0 = pl.multiple_of(g * SUBLANES, SUBLANES)
        hr = b_ref[pl.ds(r0, SUBLANES), :] + a_ref[pl.ds(r0, SUBLANES), :] * hprev
        b_ref[pl.ds(r0, SUBLANES), :] = hr
        return hr[SUBLANES - 1:SUBLANES, :]

    hcar_ref[...] = lax.fori_loop(0, ng, group_step, hcar_ref[...], unroll=4)

    y = _gelu_tanh(rg_ref[...].astype(F32)) * b_ref[...]
    out_ref[...] = _rms(y, nw_ref[...]).astype(BF16)


def _rglru(proj, cw, cb, wri, bri, lam, nw, B, S):
    T = B * S
    ts = TS_SEQ
    nst = S // ts

    def col(j):
        return pl.BlockSpec((ts, R_WIDTH), lambda b, s, j=j: (b * nst + s, j))

    def const(shape):
        return pl.BlockSpec(shape, lambda b, s: (0,) * len(shape))

    return pl.pallas_call(
        _rglru_kernel,
        grid=(B, nst),
        in_specs=[col(3), col(4), const((CONV_WIDTH, R_WIDTH)), const((1, R_WIDTH)),
                  const((R_HEADS, R_BLOCK, 2 * R_BLOCK)), const((R_HEADS, 1, 2 * R_BLOCK)),
                  const((1, R_WIDTH)), const((1, R_WIDTH))],
        out_specs=pl.BlockSpec((ts, R_WIDTH), lambda b, s: (b * nst + s, 0)),
        out_shape=jax.ShapeDtypeStruct((T, R_WIDTH), BF16),
        scratch_shapes=[
            pltpu.VMEM((ts + SUBLANES, R_WIDTH), F32),
            pltpu.VMEM((ts, R_WIDTH), F32),
            pltpu.VMEM((ts, R_WIDTH), F32),
            pltpu.VMEM((1, R_WIDTH), F32),
        ],
        compiler_params=_cparams(("arbitrary", "arbitrary")),
        name="rglru",
    )(proj, proj, cw, cb, wri, bri, lam, nw)


def _outproj_kernel(hm_ref, hr_ref, x_ref, wo_ref, n2_ref, rw2_ref, rb_ref,
                    h1_ref, xn_ref, lg_ref):
    mix = jnp.dot(hm_ref[...], wo_ref[0:M_WIDTH, :], preferred_element_type=F32)
    mix = mix + jnp.dot(hr_ref[...], wo_ref[M_WIDTH:, :], preferred_element_type=F32)
    h1 = x_ref[...] + mix
    h1_ref[...] = h1
    xn = _rms(h1, n2_ref[...])
    xh = xn.astype(BF16)
    xn_ref[...] = xh
    xl = (xn - xh.astype(F32)).astype(BF16)
    lg2 = jnp.dot(xh, rw2_ref[...], preferred_element_type=F32)
    lg = lg2[:, :LANES] + lg2[:, LANES:]
    lg = lg + jnp.dot(xl, rw2_ref[:, :LANES], preferred_element_type=F32)
    lg_ref[...] = lg.T[0:N_EXPERTS] + rb_ref[...]


def _outproj(hm, hr, x2, wo, n2, rw2, rb):
    T = x2.shape[0]
    tm = TM_PROJ

    def const(shape):
        return pl.BlockSpec(shape, lambda i: (0,) * len(shape))

    return pl.pallas_call(
        _outproj_kernel,
        grid=(T // tm,),
        in_specs=[
            pl.BlockSpec((tm, M_WIDTH), lambda i: (i, 0)),
            pl.BlockSpec((tm, R_WIDTH), lambda i: (i, 0)),
            pl.BlockSpec((tm, D_MODEL), lambda i: (i, 0)),
            const((D_MODEL, D_MODEL)), const((1, D_MODEL)),
            const((D_MODEL, 2 * LANES)), const((N_EXPERTS, 1)),
        ],
        out_specs=[
            pl.BlockSpec((tm, D_MODEL), lambda i: (i, 0)),
            pl.BlockSpec((tm, D_MODEL), lambda i: (i, 0)),
            pl.BlockSpec((N_EXPERTS, tm), lambda i: (0, i)),
        ],
        out_shape=[
            jax.ShapeDtypeStruct((T, D_MODEL), F32),
            jax.ShapeDtypeStruct((T, D_MODEL), BF16),
            jax.ShapeDtypeStruct((N_EXPERTS, T), F32),
        ],
        compiler_params=_cparams(("arbitrary",)),
        name="outproj",
    )(hm, hr, x2, wo, n2, rw2, rb)


def _route_kernel(lg_ref, ids_ref, rank_ref, meta_ref, cnt_ref, carry_ref):
    tn = lg_ref.shape[1]
    sub = 256

    @pl.when(pl.program_id(0) == 0)
    def _():
        carry_ref[...] = jnp.zeros_like(carry_ref)

    logits = lg_ref[...]
    e_idx = lax.broadcasted_iota(I32, (N_EXPERTS, tn), 0)
    vals, idxs, sels = [], [], []
    for _ in range(TOP_K):
        m = jnp.max(logits, axis=0, keepdims=True)
        idx = jnp.min(jnp.where(logits == m, e_idx, N_EXPERTS), axis=0, keepdims=True)
        sel = e_idx == idx
        logits = jnp.where(sel, -jnp.inf, logits)
        vals.append(m)
        idxs.append(idx)
        sels.append(sel)
    exps = [jnp.exp(v - vals[0]) for v in vals]
    denom = exps[0] + exps[1] + exps[2] + exps[3]
    chosen = (sels[0] | sels[1] | sels[2] | sels[3]).astype(BF16)

    s_i = lax.broadcasted_iota(I32, (sub, sub), 0)
    t_i = lax.broadcasted_iota(I32, (sub, sub), 1)
    upper = (s_i < t_i).astype(BF16)
    carry = carry_ref[:, 0:1]
    parts = []
    for c in range(tn // sub):
        mc = chosen[:, c * sub:(c + 1) * sub]
        cs = jnp.dot(mc, upper, preferred_element_type=F32) + carry
        parts.append(cs)
        carry = cs[:, sub - 1:sub] + mc[:, sub - 1:sub].astype(F32)
    before = jnp.concatenate(parts, axis=1)
    carry_ref[...] = jnp.broadcast_to(carry, carry_ref.shape)
    cnt_ref[...] = jnp.broadcast_to(carry, cnt_ref.shape)

    zeros_i = jnp.zeros((SUBLANES - TOP_K, tn), I32)
    ranks = [jnp.sum(jnp.where(s, before, 0.0), axis=0, keepdims=True).astype(I32) for s in sels]
    ids_ref[...] = jnp.concatenate(idxs + [zeros_i], axis=0)
    rank_ref[...] = jnp.concatenate(ranks + [zeros_i], axis=0)
    tok = pl.program_id(0) * tn + lax.broadcasted_iota(I32, (1, tn), 1)
    gates = [e / denom for e in exps]
    gate_hi = [g.astype(BF16).astype(F32) for g in gates]
    gate_lo = [g - h for g, h in zip(gates, gate_hi)]
    meta_ref[...] = jnp.concatenate(
        [(tok // 256).astype(F32), (tok % 256).astype(F32)]
        + [i.astype(F32) for i in idxs] + gate_hi + gate_lo
        + [jnp.ones((1, tn), F32), jnp.zeros((META_ROWS - 3 - 3 * TOP_K, tn), F32)], axis=0)


def _route(logits_t):
    T = logits_t.shape[1]
    tn = TN_ROUTE
    tok = pl.BlockSpec((SUBLANES, tn), lambda i: (0, i))
    return pl.pallas_call(
        _route_kernel,
        grid=(T // tn,),
        in_specs=[pl.BlockSpec((N_EXPERTS, tn), lambda i: (0, i))],
        out_specs=[tok, tok, pl.BlockSpec((META_ROWS, tn), lambda i: (0, i)),
                   pl.BlockSpec((N_EXPERTS, LANES), lambda i: (0, 0))],
        out_shape=[
            jax.ShapeDtypeStruct((SUBLANES, T), I32),
            jax.ShapeDtypeStruct((SUBLANES, T), I32),
            jax.ShapeDtypeStruct((META_ROWS, T), F32),
            jax.ShapeDtypeStruct((N_EXPERTS, LANES), F32),
        ],
        scratch_shapes=[pltpu.VMEM((N_EXPERTS, LANES), F32)],
        compiler_params=_cparams(("arbitrary",)),
        name="route",
    )(logits_t)


def _dest_kernel(pst_ref, ids_ref, rank_ref, dest_ref):
    ids = ids_ref[...]
    start = jnp.zeros(ids.shape, I32)
    for e in range(N_EXPERTS):
        start = jnp.where(ids == e, pst_ref[e], start)
    dest = rank_ref[...] + start
    for j in range(dest_ref.shape[0]):
        dest_ref[j] = dest[:, j * TT_ROWS:(j + 1) * TT_ROWS]


def _dest(pstarts, ids, rank):
    T = ids.shape[1]
    tn = TN_ROUTE
    per = tn // TT_ROWS
    tok = pl.BlockSpec((SUBLANES, tn), lambda i, pst: (0, i))
    return pl.pallas_call(
        _dest_kernel,
        grid_spec=pltpu.PrefetchScalarGridSpec(
            num_scalar_prefetch=1,
            grid=(T // tn,),
            in_specs=[tok, tok],
            out_specs=pl.BlockSpec((per, SUBLANES, TT_ROWS), lambda i, pst: (i, 0, 0)),
        ),
        out_shape=jax.ShapeDtypeStruct((T // TT_ROWS, SUBLANES, TT_ROWS), I32),
        compiler_params=_cparams(("arbitrary",)),
        name="dest",
    )(pstarts, ids, rank)


def _rows_to_tiles(cols):
    return jnp.swapaxes(jnp.stack(cols, axis=0), 0, 1)


def _tiles_to_rows(tiles):
    return jnp.swapaxes(tiles, 0, 1)


def _dispatch_kernel(cnt_ref, pst_ref, pad_ref, nb_ref, dest_hbm, x_ref, meta_ref, xs_hbm,
                     dsm, ext, zbuf, sem_d, sem_x, sem_z):
    tt = x_ref.shape[0]
    i = pl.program_id(0)
    n = pl.num_programs(0)
    slot = i % 2
    n_blocks = xs_hbm.shape[0] // TM_MOE

    def dest_copy(j, sl):
        return pltpu.make_async_copy(dest_hbm.at[j], dsm.at[sl], sem_d.at[sl])

    def zero_copy(r):
        return pltpu.make_async_copy(zbuf.at[0], xs_hbm.at[r], sem_z)

    def zero_block_copy(b):
        r0 = pl.multiple_of(b * TM_MOE, TM_MOE)
        return pltpu.make_async_copy(zbuf, xs_hbm.at[pl.ds(r0, TM_MOE)], sem_z)

    def rows_wait(sl):
        for _ in range(TOP_K):
            pltpu.make_async_copy(ext.at[sl], xs_hbm.at[pl.ds(0, tt)], sem_x.at[sl]).wait()

    def for_each_pad_row(fn):
        def per_expert(e, carry):
            lo = pst_ref[e] + cnt_ref[e]
            hi = pst_ref[e] + pad_ref[e]

            def per_row(r, c2):
                fn(r)
                return c2

            return lax.fori_loop(lo, hi, per_row, carry)

        lax.fori_loop(0, N_EXPERTS, per_expert, 0)

    def for_each_unused_block(fn):
        def per_block(b, carry):
            fn(b)
            return carry

        lax.fori_loop(nb_ref[0], n_blocks, per_block, 0)

    @pl.when(i == 0)
    def _():
        dest_copy(0, 0).start()
        zbuf[...] = jnp.zeros_like(zbuf)
        for_each_pad_row(lambda r: zero_copy(r).start())
        for_each_unused_block(lambda b: zero_block_copy(b).start())

    x = x_ref[...].astype(F32)
    cols = [x[:, j * LANES:(j + 1) * LANES] for j in range(D_MODEL // LANES)]
    meta_pad = jnp.concatenate([meta_ref[...], jnp.zeros((LANES - META_ROWS, tt), F32)], axis=0)
    cols.append(meta_pad.T)
    cols += [jnp.zeros((tt, LANES), F32)] * (ROW_SUBLANES - len(cols))
    ext[slot] = _rows_to_tiles(cols).astype(BF16)

    dest_copy(i, slot).wait()

    @pl.when(i + 1 < n)
    def _():
        dest_copy(i + 1, 1 - slot).start()

    group = 16
    for t0 in range(0, tt, group):
        dsts = [[dsm[slot, k, t] for k in range(TOP_K)] for t in range(t0, t0 + group)]
        for j, t in enumerate(range(t0, t0 + group)):
            for k in range(TOP_K):
                pltpu.make_async_copy(ext.at[slot, t], xs_hbm.at[dsts[j][k]],
                                      sem_x.at[slot]).start(priority=k % 2)

    @pl.when(i == 0)
    def _():
        for_each_pad_row(lambda r: zero_copy(r).wait())
        for_each_unused_block(lambda b: zero_block_copy(b).wait())

    @pl.when(i > 0)
    def _():
        rows_wait(1 - slot)

    @pl.when(i == n - 1)
    def _():
        rows_wait(slot)


def _dispatch(counts, pstarts, padded, nb_used, dest_tiles, xn, meta, n_rows):
    T = xn.shape[0]
    tt = TT_ROWS
    return pl.pallas_call(
        _dispatch_kernel,
        grid_spec=pltpu.PrefetchScalarGridSpec(
            num_scalar_prefetch=4,
            grid=(T // tt,),
            in_specs=[
                pl.BlockSpec(memory_space=pl.ANY),
                pl.BlockSpec((tt, D_MODEL), lambda i, *_: (i, 0)),
                pl.BlockSpec((META_ROWS, tt), lambda i, *_: (0, i)),
            ],
            out_specs=pl.BlockSpec(memory_space=pl.ANY),
            scratch_shapes=[
                pltpu.SMEM((2, SUBLANES, tt), I32),
                pltpu.VMEM((2, tt, ROW_SUBLANES, LANES), BF16),
                pltpu.VMEM((TM_MOE, ROW_SUBLANES, LANES), BF16),
                pltpu.SemaphoreType.DMA((2,)),
                pltpu.SemaphoreType.DMA((2,)),
                pltpu.SemaphoreType.DMA,
            ],
        ),
        out_shape=jax.ShapeDtypeStruct((n_rows, ROW_SUBLANES, LANES), BF16),
        compiler_params=_cparams(("arbitrary",)),
        name="dispatch",
    )(counts, pstarts, padded, nb_used, dest_tiles, xn, meta)


def _experts_kernel(be_ref, nb_ref, xs_ref, wgu_ref, bgu_ref, wd_ref, bd_ref, out_hbm,
                    wgu_bf, wd_bf, x_bf, acc_ref, ybuf, code_v, code_s, sem_c, sem_y):
    tm = TM_MOE
    b = pl.program_id(0)
    slot = b % 2
    nbu = nb_ref[0]
    n_tok = (out_hbm.shape[0] - 2 * tm) // TOP_K
    dump0 = TOP_K * n_tok

    def scatter_wait(sl):
        pltpu.make_async_copy(ybuf.at[sl], out_hbm.at[pl.ds(0, tm)], sem_y.at[sl]).wait()

    def scatter_row(sl, r, dst):
        return pltpu.make_async_copy(ybuf.at[sl, r], out_hbm.at[dst], sem_y.at[sl])

    def code_copy(sl):
        return pltpu.make_async_copy(code_v, code_s.at[sl], sem_c)

    @pl.when(b == 0)
    def _():
        ybuf[...] = jnp.zeros_like(ybuf)
        for sl in range(2):
            cp = pltpu.make_async_copy(ybuf.at[sl], out_hbm.at[pl.ds(dump0 + sl * tm, tm)],
                                       sem_y.at[sl])
            cp.start()
            cp.wait()

        def init_code(r, carry):
            code_s[1, 0, r] = dump0 + tm + r
            return carry

        lax.fori_loop(0, tm, init_code, 0)

    @pl.when(b < nbu)
    def _():
        e = be_ref[b]
        prev = be_ref[jnp.maximum(b - 1, 0)]

        @pl.when((b == 0) | (e != prev))
        def _():
            wgu_bf[...] = wgu_ref[0].astype(BF16)
            wd_bf[...] = wd_ref[0].astype(BF16)

        n_chunks = D_FF // FF_CHUNK
        rows_per_chunk = tm // n_chunks

        def scatter_prev_rows(c):
            @pl.when(b >= 0)
            def _():
                group = 16
                for r0 in range(c * rows_per_chunk, (c + 1) * rows_per_chunk, group):
                    dsts = [code_s[1 - slot, 0, r] for r in range(r0, r0 + group)]
                    for j, r in enumerate(range(r0, r0 + group)):
                        scatter_row(1 - slot, r, dsts[j]).start(priority=r % 2)

        planes = _tiles_to_rows(xs_ref[...].astype(F32))
        meta = planes[D_MODEL // LANES]
        e_f = e.astype(F32)
        tok = meta[:, 0:1] * 256.0 + meta[:, 1:2]
        valid = meta[:, 2 + 3 * TOP_K:3 + 3 * TOP_K] > 0.5
        choice = jnp.zeros((tm, 1), F32)
        gate = jnp.zeros((tm, 1), F32)
        for j in range(TOP_K):
            hit = meta[:, 2 + j:3 + j] == e_f
            choice = jnp.where(hit, float(j), choice)
            gate_j = meta[:, 2 + TOP_K + j:3 + TOP_K + j] + meta[:, 2 + 2 * TOP_K + j:3 + 2 * TOP_K + j]
            gate = jnp.where(hit, gate_j, gate)
        spare = (dump0 + slot * tm + lax.broadcasted_iota(I32, (tm, 1), 0)).astype(F32)
        code = jnp.where(valid, choice * float(n_tok) + tok, spare)
        code_row = jnp.broadcast_to(code, (tm, LANES)).T[0:1, :]
        code_v[...] = code_row.astype(I32)
        code_copy(slot).start()

        x_bf[...] = jnp.concatenate([planes[j].astype(BF16) for j in range(D_MODEL // LANES)],
                                    axis=1)

        for c in range(n_chunks):
            gcols = slice(c * FF_CHUNK, (c + 1) * FF_CHUNK)
            ucols = slice(D_FF + c * FF_CHUNK, D_FF + (c + 1) * FF_CHUNK)
            x = x_bf[...]
            g = jnp.dot(x, wgu_bf[:, gcols], preferred_element_type=F32) + bgu_ref[0, :, gcols]
            up = jnp.dot(x, wgu_bf[:, ucols], preferred_element_type=F32) + bgu_ref[0, :, ucols]
            g = jnp.minimum(g, SWIGLU_LIMIT)
            up = jnp.clip(up, -SWIGLU_LIMIT, SWIGLU_LIMIT)
            hdn = ((up + 1.0) * g * jax.nn.sigmoid(g * SWIGLU_ALPHA)).astype(BF16)
            part = jnp.dot(hdn, wd_bf[gcols, :], preferred_element_type=F32)
            if c == 0:
                acc_ref[...] = part
            else:
                acc_ref[...] += part
            scatter_prev_rows(c)

        @pl.when(b > 0)
        def _():
            scatter_wait(slot)

        y = (acc_ref[...] + bd_ref[0]) * gate
        ybuf[slot] = _rows_to_tiles([y[:, j * LANES:(j + 1) * LANES]
                                     for j in range(D_MODEL // LANES)])
        code_copy(slot).wait()

    @pl.when(b == nbu)
    def _():
        scatter_wait(slot)

        def per_row(r, carry):
            scatter_row(1 - slot, r, code_s[1 - slot, 0, r]).start()
            return carry

        lax.fori_loop(0, tm, per_row, 0, unroll=8)
        scatter_wait(1 - slot)


def _experts(block_e, nb_used, xs, wgu, bgu, wd, bd, n_tok):
    P = xs.shape[0]
    tm = TM_MOE
    nb = P // tm

    def row_block(b, be, nbu):
        return (jnp.minimum(b, nbu[0] - 1), 0, 0)

    def expert_block(b, be, nbu):
        return (be[jnp.minimum(b, nbu[0] - 1)], 0, 0)

    return pl.pallas_call(
        _experts_kernel,
        grid_spec=pltpu.PrefetchScalarGridSpec(
            num_scalar_prefetch=2,
            grid=(nb,),
            in_specs=[
                pl.BlockSpec((tm, ROW_SUBLANES, LANES), row_block),
                pl.BlockSpec((1, D_MODEL, 2 * D_FF), expert_block),
                pl.BlockSpec((1, 1, 2 * D_FF), expert_block),
                pl.BlockSpec((1, D_FF, D_MODEL), expert_block),
                pl.BlockSpec((1, 1, D_MODEL), expert_block),
            ],
            out_specs=pl.BlockSpec(memory_space=pl.ANY),
            scratch_shapes=[
                pltpu.VMEM((D_MODEL, 2 * D_FF), BF16),
                pltpu.VMEM((D_FF, D_MODEL), BF16),
                pltpu.VMEM((tm, D_MODEL), BF16),
                pltpu.VMEM((tm, D_MODEL), F32),
                pltpu.VMEM((2, tm, SUBLANES, LANES), F32),
                pltpu.VMEM((1, tm), I32),
                pltpu.SMEM((2, 1, tm), I32),
                pltpu.SemaphoreType.DMA,
                pltpu.SemaphoreType.DMA((2,)),
            ],
        ),
        out_shape=jax.ShapeDtypeStruct((TOP_K * n_tok + 2 * tm, SUBLANES, LANES), F32),
        compiler_params=_cparams(("arbitrary",)),
        name="experts",
    )(block_e, nb_used, xs, wgu, bgu, wd, bd)


def _combine_kernel(h1_ref, p0_ref, p1_ref, p2_ref, p3_ref, fw_ref, out_ref):
    moe_tiles = p0_ref[...] + p1_ref[...] + p2_ref[...] + p3_ref[...]
    moe_cols = _tiles_to_rows(moe_tiles)
    moe = jnp.concatenate([moe_cols[j] for j in range(D_MODEL // LANES)], axis=1)
    out_ref[...] = _rms(h1_ref[...] + moe, fw_ref[...])


def _combine(h1, planes, fw):
    T = h1.shape[0]
    tt = TT_COMBINE
    nt = T // tt

    def plane(k):
        return pl.BlockSpec((tt, SUBLANES, LANES), lambda i, k=k: (k * nt + i, 0, 0))

    return pl.pallas_call(
        _combine_kernel,
        grid=(nt,),
        in_specs=[pl.BlockSpec((tt, D_MODEL), lambda i: (i, 0)),
                  plane(0), plane(1), plane(2), plane(3),
                  pl.BlockSpec((1, D_MODEL), lambda i: (0, 0))],
        out_specs=pl.BlockSpec((tt, D_MODEL), lambda i: (i, 0)),
        out_shape=jax.ShapeDtypeStruct((T, D_MODEL), F32),
        compiler_params=_cparams(("arbitrary",)),
        name="combine",
    )(h1, planes, planes, planes, planes, fw)


def kernel(x, norm1_w, w_in, mlstm_gate_b, mlstm_norm_w, conv_w, conv_b, lru_wr, lru_br, lru_wi,
           lru_bi, lru_lambda, lru_norm_w, w_out, norm2_w, router_w, router_b, moe_w_gu, moe_b_gu,
           moe_w_down, moe_b_down, final_norm_w):
    B, S, D = x.shape
    assert D == D_MODEL and norm1_w.shape[0] == 1
    assert S % TS_SEQ == 0 and (B * S) % TN_ROUTE == 0
    T = B * S
    x2 = x.reshape(T, D)

    w = w_in[0]
    k_off = M_WIDTH
    v_off = 2 * M_WIDTH
    g_off = 3 * M_WIDTH
    o_off = g_off + 2 * M_HEADS
    w_main = jnp.concatenate([w[:, :k_off], w[:, v_off:g_off], w[:, o_off:]], axis=1).astype(BF16)
    w_kt = w[:, k_off:v_off].T.astype(BF16)
    w_gate = jnp.zeros((D, LANES), F32)
    w_gate = w_gate.at[:, 0:M_HEADS].set(w[:, g_off:g_off + M_HEADS])
    w_gate = w_gate.at[:, SUBLANES:SUBLANES + M_HEADS].set(w[:, g_off + M_HEADS:o_off])
    gate_b = jnp.zeros((LANES, 1), F32)
    gate_b = gate_b.at[0:M_HEADS, 0].set(mlstm_gate_b[0, :M_HEADS])
    gate_b = gate_b.at[SUBLANES:SUBLANES + M_HEADS, 0].set(mlstm_gate_b[0, M_HEADS:])
    wri = (0.5 * jnp.concatenate([lru_wr[0], lru_wi[0]], axis=-1)).astype(BF16)
    bri = 0.5 * jnp.concatenate([lru_br[0].reshape(R_HEADS, 1, R_BLOCK),
                                 lru_bi[0].reshape(R_HEADS, 1, R_BLOCK)], axis=-1)
    rw = jnp.zeros((D, LANES), F32).at[:, :N_EXPERTS].set(router_w[0])
    rwh = rw.astype(BF16)
    rwl = (rw - rwh.astype(F32)).astype(BF16)
    rw2 = jnp.concatenate([rwh, rwl], axis=1)

    proj, kt, g_row, rt = _inproj(x2, norm1_w, w_main, w_kt, w_gate.astype(BF16), gate_b)
    h_m = _mlstm(proj, kt, g_row, rt, mlstm_norm_w, B, S)
    h_r = _rglru(proj, conv_w[0], conv_b, wri, bri, lru_lambda, lru_norm_w, B, S)
    h1, xn2, logits_t = _outproj(h_m, h_r, x2, w_out[0].astype(BF16), norm2_w, rw2,
                                 router_b[0].reshape(N_EXPERTS, 1))

    ids, rank, meta, cnt = _route(logits_t)
    counts = cnt[:, 0].astype(I32)
    padded = ((counts + TM_MOE - 1) // TM_MOE) * TM_MOE
    pends = jnp.cumsum(padded)
    pstarts = pends - padded
    nb = (T * TOP_K) // TM_MOE + N_EXPERTS
    nb_used = (pends[-1:] // TM_MOE).astype(I32)
    block_start = jnp.arange(nb, dtype=I32) * TM_MOE
    block_e = jnp.minimum(jnp.sum((pends[None, :] <= block_start[:, None]).astype(I32), axis=1),
                          N_EXPERTS - 1)

    dest_tiles = _dest(pstarts, ids, rank)
    xs = _dispatch(counts, pstarts, padded, nb_used, dest_tiles, xn2, meta, nb * TM_MOE)
    planes = _experts(block_e, nb_used, xs, moe_w_gu[0],
                      moe_b_gu[0].reshape(N_EXPERTS, 1, 2 * D_FF),
                      moe_w_down[0], moe_b_down[0].reshape(N_EXPERTS, 1, D), T)
    out = _combine(h1, planes, final_norm_w.reshape(1, D))
    return out.reshape(B, S, D)
```

```python
import jax
import jax.numpy as jnp
from jax import lax
from jax.experimental import pallas as pl
from jax.experimental.pallas import tpu as pltpu

F32 = jnp.float32
BF16 = jnp.bfloat16
I32 = jnp.int32

D_MODEL = 1024
M_HEADS = 4
HEAD_DIM = 128
M_WIDTH = M_HEADS * HEAD_DIM
R_HEADS = 4
R_BLOCK = 128
R_WIDTH = R_HEADS * R_BLOCK
CONV_WIDTH = 4
LRU_C = 8.0
N_EXPERTS = 32
TOP_K = 4
D_FF = 1024
SWIGLU_LIMIT = 7.0
SWIGLU_ALPHA = 1.702
EPS = 1e-6

N_MAIN = 5 * 512
LANES = 128
SUBLANES = 8
VMEM_LIMIT = 56 * 1024 * 1024

TM_PROJ = 512
TS_SEQ = 512
CHUNK = 128
TN_ROUTE = 2048
TT_ROWS = 256
TT_COMBINE = 512
TM_MOE = 512
FF_CHUNK = 512
META_ROWS = 16
ROW_SUBLANES = 16
PLANE_DTYPE = jnp.bfloat16
NEG_BIG = -1e30


def _cparams(sem, vmem=VMEM_LIMIT):
    return pltpu.CompilerParams(dimension_semantics=sem, vmem_limit_bytes=vmem)


def _rms(x, w):
    ms = jnp.mean(x * x, axis=-1, keepdims=True)
    return x * lax.rsqrt(ms + EPS) * w


def _sigmoid(x):
    return 0.5 * jnp.tanh(0.5 * x) + 0.5


def _segmented_scan(x, op, ident, lane_in_chunk):
    d = 1
    while d < CHUNK:
        shifted = pltpu.roll(x, d, axis=1)
        x = op(x, jnp.where(lane_in_chunk >= d, shifted, ident))
        d *= 2
    return x


def _inproj_kernel(x_ref, nw_ref, w_ref, wkt_ref, wg_ref, gb_ref,
                   proj_ref, kt_ref, grow_ref, rt_ref):
    tm = x_ref.shape[0]
    xn = _rms(x_ref[...], nw_ref[...]).astype(BF16)

    g = jnp.dot(xn, wg_ref[...], preferred_element_type=F32)
    gt = g.T + gb_ref[...]
    ig = gt[0:SUBLANES]
    fg = gt[SUBLANES:2 * SUBLANES]
    lf = jnp.minimum(fg, 0.0) - jnp.log1p(jnp.exp(-jnp.abs(fg)))
    lane_in_chunk = lax.broadcasted_iota(I32, (SUBLANES, tm), 1) % CHUNK
    b_row = _segmented_scan(lf, jnp.add, 0.0, lane_in_chunk)
    g_row = ig - b_row
    cm_row = _segmented_scan(g_row, jnp.maximum, NEG_BIG, lane_in_chunk)
    grow_ref[...] = g_row
    stacked = jnp.concatenate(
        [b_row, g_row, cm_row, jnp.zeros((LANES - 3 * SUBLANES, tm), F32)], axis=0)
    rt_ref[...] = stacked.T

    for c in range(N_MAIN // 512):
        cols = slice(c * 512, (c + 1) * 512)
        proj_ref[:, cols] = jnp.dot(xn, w_ref[:, cols], preferred_element_type=F32).astype(BF16)

    kt_ref[...] = lax.dot_general(wkt_ref[...], xn, (((1,), (1,)), ((), ())),
                                  preferred_element_type=F32).astype(BF16)


def _inproj(x2, nw, w_main, w_kt, w_gate, gate_b):
    T = x2.shape[0]
    tm = TM_PROJ

    def const(shape):
        return pl.BlockSpec(shape, lambda i: (0,) * len(shape))

    return pl.pallas_call(
        _inproj_kernel,
        grid=(T // tm,),
        in_specs=[
            pl.BlockSpec((tm, D_MODEL), lambda i: (i, 0)),
            const((1, D_MODEL)), const((D_MODEL, N_MAIN)), const((M_WIDTH, D_MODEL)),
            const((D_MODEL, LANES)), const((LANES, 1)),
        ],
        out_specs=[
            pl.BlockSpec((tm, N_MAIN), lambda i: (i, 0)),
            pl.BlockSpec((M_WIDTH, tm), lambda i: (0, i)),
            pl.BlockSpec((SUBLANES, tm), lambda i: (0, i)),
            pl.BlockSpec((tm, LANES), lambda i: (i, 0)),
        ],
        out_shape=[
            jax.ShapeDtypeStruct((T, N_MAIN), BF16),
            jax.ShapeDtypeStruct((M_WIDTH, T), BF16),
            jax.ShapeDtypeStruct((SUBLANES, T), F32),
            jax.ShapeDtypeStruct((T, LANES), F32),
        ],
        compiler_params=_cparams(("arbitrary",)),
        name="inproj",
    )(x2, nw, w_main, w_kt, w_gate, gate_b)


def _mlstm_kernel(q_ref, v_ref, o_ref, kt_ref, grow_ref, rt_ref, nw_ref, out_ref,
                  c_ref, m_ref, bc_ref):
    ts = q_ref.shape[0]
    scale = HEAD_DIM ** -0.5

    @pl.when(pl.program_id(1) == 0)
    def _():
        c_ref[...] = jnp.zeros_like(c_ref)
        m_ref[...] = jnp.zeros_like(m_ref)

    for idx, term in enumerate((0, 2)):
        for h in range(M_HEADS):
            col = term * SUBLANES + h
            bc_ref[idx, h] = jnp.broadcast_to(rt_ref[:, col:col + 1], (ts, LANES))

    t_idx = lax.broadcasted_iota(I32, (CHUNK, CHUNK), 0)
    s_idx = lax.broadcasted_iota(I32, (CHUNK, CHUNK), 1)
    causal = t_idx >= s_idx
    ones_blk = jnp.ones((CHUNK, HEAD_DIM), BF16)

    for c in range(ts // CHUNK):
        rows = slice(c * CHUNK, (c + 1) * CHUNK)
        last = slice((c + 1) * CHUNK - 1, (c + 1) * CHUNK)
        for h in range(M_HEADS):
            cols = slice(h * HEAD_DIM, (h + 1) * HEAD_DIM)
            qc = q_ref[rows, cols]
            kt = kt_ref[cols, rows]
            v_ext = jnp.concatenate([v_ref[rows, cols], ones_blk], axis=1)
            b_bc = bc_ref[0, h, rows, :]
            cm_bc = bc_ref[1, h, rows, :]
            g_r = grow_ref[h:h + 1, rows]
            m_prev = m_ref[h:h + 1, :]
            c_prev = c_ref[h]

            mx = jnp.maximum(cm_bc, m_prev)
            decay_mat = jnp.exp(jnp.where(causal, g_r - mx, NEG_BIG))
            scores = jnp.dot(qc, kt, preferred_element_type=F32)
            p = (scores * decay_mat * scale).astype(BF16)
            intra = jnp.dot(p, v_ext, preferred_element_type=F32)
            inter = jnp.dot(qc, c_prev.astype(BF16), preferred_element_type=F32)
            w_inter = jnp.exp(m_prev - mx)
            num = intra[:, :HEAD_DIM] + w_inter * inter[:, :HEAD_DIM]
            den = intra[:, HEAD_DIM:] + w_inter * inter[:, HEAD_DIM:]
            m_t = b_bc + mx
            h_t = num / jnp.maximum(jnp.abs(den), jnp.exp(-m_t))

            mx_last = jnp.maximum(bc_ref[1, h, last, :], m_prev)
            w_row = jnp.exp(g_r - mx_last) * scale
            kw_t = (kt.astype(F32) * w_row).astype(BF16)
            upd = jnp.dot(kw_t, v_ext, preferred_element_type=F32)
            dec = jnp.exp(m_prev - mx_last)
            c_ref[h] = jnp.concatenate([dec, dec], axis=1) * c_prev + upd
            m_ref[h:h + 1, :] = bc_ref[0, h, last, :] + mx_last

            y = _sigmoid(o_ref[rows, cols].astype(F32)) * h_t
            y2 = y * y
            y2_hi = y2.astype(BF16)
            y2_lo = (y2 - y2_hi.astype(F32)).astype(BF16)
            ss = (jnp.dot(y2_hi, ones_blk, preferred_element_type=F32)
                  + jnp.dot(y2_lo, ones_blk, preferred_element_type=F32))
            out_ref[rows, cols] = (y * lax.rsqrt(ss * (1.0 / HEAD_DIM) + EPS)
                                   * nw_ref[:, cols]).astype(BF16)


def _mlstm(proj, kt, g_row, rt, nw, B, S):
    T = B * S
    ts = TS_SEQ
    nst = S // ts

    def col(j):
        return pl.BlockSpec((ts, M_WIDTH), lambda b, s, j=j: (b * nst + s, j))

    return pl.pallas_call(
        _mlstm_kernel,
        grid=(B, nst),
        in_specs=[col(0), col(1), col(2),
                  pl.BlockSpec((M_WIDTH, ts), lambda b, s: (0, b * nst + s)),
                  pl.BlockSpec((SUBLANES, ts), lambda b, s: (0, b * nst + s)),
                  pl.BlockSpec((ts, LANES), lambda b, s: (b * nst + s, 0)),
                  pl.BlockSpec((1, M_WIDTH), lambda b, s: (0, 0))],
        out_specs=pl.BlockSpec((ts, M_WIDTH), lambda b, s: (b * nst + s, 0)),
        out_shape=jax.ShapeDtypeStruct((T, M_WIDTH), BF16),
        scratch_shapes=[
            pltpu.VMEM((M_HEADS, HEAD_DIM, 2 * HEAD_DIM), F32),
            pltpu.VMEM((SUBLANES, LANES), F32),
            pltpu.VMEM((2, M_HEADS, ts, LANES), F32),
        ],
        compiler_params=_cparams(("arbitrary", "arbitrary")),
        name="mlstm",
    )(proj, proj, proj, kt, g_row, rt, nw)


def _gelu_tanh(x):
    return 0.5 * x * (1.0 + jnp.tanh(0.7978845608028654 * (x + 0.044715 * (x * x * x))))


def _rglru_kernel(rx_ref, rg_ref, cw_ref, cb_ref, wri_ref, bri_ref, lam_ref, nw_ref, out_ref,
                  xpad_ref, a_ref, b_ref, hcar_ref):
    ts = rx_ref.shape[0]
    ng = ts // SUBLANES

    @pl.when(pl.program_id(1) == 0)
    def _():
        xpad_ref[0:SUBLANES, :] = jnp.zeros((SUBLANES, R_WIDTH), F32)
        hcar_ref[...] = jnp.zeros_like(hcar_ref)

    x = rx_ref[...].astype(F32)
    xpad_ref[SUBLANES:SUBLANES + ts, :] = x
    xp = xpad_ref[...]
    xc = cb_ref[...]
    for j in range(CONV_WIDTH):
        sh = CONV_WIDTH - 1 - j
        tap = xp if sh == 0 else pltpu.roll(xp, sh, axis=0)
        xc = xc + cw_ref[j:j + 1, :] * tap[SUBLANES:SUBLANES + ts, :]
    xpad_ref[0:SUBLANES, :] = x[ts - SUBLANES:ts, :]

    nlam = -lam_ref[...]
    softplus_nlam = jnp.maximum(nlam, 0.0) + jnp.log1p(jnp.exp(-jnp.abs(nlam)))
    row_in_group = lax.broadcasted_iota(I32, (ng, SUBLANES, R_BLOCK), 1)

    for h in range(R_HEADS):
        cols = slice(h * R_BLOCK, (h + 1) * R_BLOCK)
        xh = xc[:, cols]
        half_z = jnp.dot(xh.astype(BF16), wri_ref[h], preferred_element_type=F32) + bri_ref[h]
        t_r = jnp.tanh(half_z[:, :R_BLOCK])
        t_i = jnp.tanh(half_z[:, R_BLOCK:])
        c_half = (-0.5 * LRU_C) * softplus_nlam[:, cols]
        log_a = c_half * t_r + c_half
        a = jnp.exp(log_a)
        one_m_a2 = -jnp.tanh(log_a) * (a * a + 1.0)
        half_x = 0.5 * xh
        bt = jnp.sqrt(one_m_a2) * (half_x * t_i + half_x)
        a = a.reshape(ng, SUBLANES, R_BLOCK)
        bt = bt.reshape(ng, SUBLANES, R_BLOCK)
        d = 1
        while d < SUBLANES:
            keep = row_in_group >= d
            a_sh = pltpu.roll(a, d, axis=1)
            b_sh = pltpu.roll(bt, d, axis=1)
            bt = bt + jnp.where(keep, a * b_sh, 0.0)
            a = a * jnp.where(keep, a_sh, 1.0)
            d *= 2
        a_ref[:, cols] = a.reshape(ts, R_BLOCK)
        b_ref[:, cols] = bt.reshape(ts, R_BLOCK)

    def group_step(g, hprev):
        r0 = pl.multiple_of(g * SUBLANES, SUBLANES)
        hr = b_ref[pl.ds(r0, SUBLANES), :] + a_ref[pl.ds(r0, SUBLANES), :] * hprev
        b_ref[pl.ds(r0, SUBLANES), :] = hr
        return hr[SUBLANES - 1:SUBLANES, :]

    hcar_ref[...] = lax.fori_loop(0, ng, group_step, hcar_ref[...], unroll=4)

    y = _gelu_tanh(rg_ref[...].astype(F32)) * b_ref[...]
    out_ref[...] = _rms(y, nw_ref[...]).astype(BF16)


def _rglru(proj, cw, cb, wri, bri, lam, nw, B, S):
    T = B * S
    ts = TS_SEQ
    nst = S // ts

    def col(j):
        return pl.BlockSpec((ts, R_WIDTH), lambda b, s, j=j: (b * nst + s, j))

    def const(shape):
        return pl.BlockSpec(shape, lambda b, s: (0,) * len(shape))

    return pl.pallas_call(
        _rglru_kernel,
        grid=(B, nst),
        in_specs=[col(3), col(4), const((CONV_WIDTH, R_WIDTH)), const((1, R_WIDTH)),
                  const((R_HEADS, R_BLOCK, 2 * R_BLOCK)), const((R_HEADS, 1, 2 * R_BLOCK)),
                  const((1, R_WIDTH)), const((1, R_WIDTH))],
        out_specs=pl.BlockSpec((ts, R_WIDTH), lambda b, s: (b * nst + s, 0)),
        out_shape=jax.ShapeDtypeStruct((T, R_WIDTH), BF16),
        scratch_shapes=[
            pltpu.VMEM((ts + SUBLANES, R_WIDTH), F32),
            pltpu.VMEM((ts, R_WIDTH), F32),
            pltpu.VMEM((ts, R_WIDTH), F32),
            pltpu.VMEM((1, R_WIDTH), F32),
        ],
        compiler_params=_cparams(("arbitrary", "arbitrary")),
        name="rglru",
    )(proj, proj, cw, cb, wri, bri, lam, nw)


def _outproj_kernel(hm_ref, hr_ref, x_ref, wo_ref, n2_ref, rw2_ref, rb_ref,
                    h1_ref, xn_ref, lg_ref):
    mix = jnp.dot(hm_ref[...], wo_ref[0:M_WIDTH, :], preferred_element_type=F32)
    mix = mix + jnp.dot(hr_ref[...], wo_ref[M_WIDTH:, :], preferred_element_type=F32)
    h1 = x_ref[...] + mix
    h1_ref[...] = h1
    xn = _rms(h1, n2_ref[...])
    xh = xn.astype(BF16)
    xn_ref[...] = xh
    xl = (xn - xh.astype(F32)).astype(BF16)
    lg2 = jnp.dot(xh, rw2_ref[...], preferred_element_type=F32)
    lg = lg2[:, :LANES] + lg2[:, LANES:]
    lg = lg + jnp.dot(xl, rw2_ref[:, :LANES], preferred_element_type=F32)
    lg_ref[...] = lg.T[0:N_EXPERTS] + rb_ref[...]


def _outproj(hm, hr, x2, wo, n2, rw2, rb):
    T = x2.shape[0]
    tm = TM_PROJ

    def const(shape):
        return pl.BlockSpec(shape, lambda i: (0,) * len(shape))

    return pl.pallas_call(
        _outproj_kernel,
        grid=(T // tm,),
        in_specs=[
            pl.BlockSpec((tm, M_WIDTH), lambda i: (i, 0)),
            pl.BlockSpec((tm, R_WIDTH), lambda i: (i, 0)),
            pl.BlockSpec((tm, D_MODEL), lambda i: (i, 0)),
            const((D_MODEL, D_MODEL)), const((1, D_MODEL)),
            const((D_MODEL, 2 * LANES)), const((N_EXPERTS, 1)),
        ],
        out_specs=[
            pl.BlockSpec((tm, D_MODEL), lambda i: (i, 0)),
            pl.BlockSpec((tm, D_MODEL), lambda i: (i, 0)),
            pl.BlockSpec((N_EXPERTS, tm), lambda i: (0, i)),
        ],
        out_shape=[
            jax.ShapeDtypeStruct((T, D_MODEL), F32),
            jax.ShapeDtypeStruct((T, D_MODEL), BF16),
            jax.ShapeDtypeStruct((N_EXPERTS, T), F32),
        ],
        compiler_params=_cparams(("arbitrary",)),
        name="outproj",
    )(hm, hr, x2, wo, n2, rw2, rb)


def _route_kernel(lg_ref, ids_ref, rank_ref, meta_ref, cnt_ref, carry_ref):
    tn = lg_ref.shape[1]
    sub = 256

    @pl.when(pl.program_id(0) == 0)
    def _():
        carry_ref[...] = jnp.zeros_like(carry_ref)

    logits = lg_ref[...]
    e_idx = lax.broadcasted_iota(I32, (N_EXPERTS, tn), 0)
    vals, idxs, sels = [], [], []
    for _ in range(TOP_K):
        m = jnp.max(logits, axis=0, keepdims=True)
        idx = jnp.min(jnp.where(logits == m, e_idx, N_EXPERTS), axis=0, keepdims=True)
        sel = e_idx == idx
        logits = jnp.where(sel, -jnp.inf, logits)
        vals.append(m)
        idxs.append(idx)
        sels.append(sel)
    exps = [jnp.exp(v - vals[0]) for v in vals]
    denom = exps[0] + exps[1] + exps[2] + exps[3]
    chosen = (sels[0] | sels[1] | sels[2] | sels[3]).astype(BF16)

    s_i = lax.broadcasted_iota(I32, (sub, sub), 0)
    t_i = lax.broadcasted_iota(I32, (sub, sub), 1)
    upper = (s_i < t_i).astype(BF16)
    carry = carry_ref[:, 0:1]
    parts = []
    for c in range(tn // sub):
        mc = chosen[:, c * sub:(c + 1) * sub]
        cs = jnp.dot(mc, upper, preferred_element_type=F32) + carry
        parts.append(cs)
        carry = cs[:, sub - 1:sub] + mc[:, sub - 1:sub].astype(F32)
    before = jnp.concatenate(parts, axis=1)
    carry_ref[...] = jnp.broadcast_to(carry, carry_ref.shape)
    cnt_ref[...] = jnp.broadcast_to(carry, cnt_ref.shape)

    zeros_i = jnp.zeros((SUBLANES - TOP_K, tn), I32)
    ranks = [jnp.sum(jnp.where(s, before, 0.0), axis=0, keepdims=True).astype(I32) for s in sels]
    ids_ref[...] = jnp.concatenate(idxs + [zeros_i], axis=0)
    rank_ref[...] = jnp.concatenate(ranks + [zeros_i], axis=0)
    tok = pl.program_id(0) * tn + lax.broadcasted_iota(I32, (1, tn), 1)
    gates = [e / denom for e in exps]
    gate_hi = [g.astype(BF16).astype(F32) for g in gates]
    gate_lo = [g - h for g, h in zip(gates, gate_hi)]
    meta_ref[...] = jnp.concatenate(
        [(tok // 256).astype(F32), (tok % 256).astype(F32)]
        + [i.astype(F32) for i in idxs] + gate_hi + gate_lo
        + [jnp.ones((1, tn), F32), jnp.zeros((META_ROWS - 3 - 3 * TOP_K, tn), F32)], axis=0)


def _route(logits_t):
    T = logits_t.shape[1]
    tn = TN_ROUTE
    tok = pl.BlockSpec((SUBLANES, tn), lambda i: (0, i))
    return pl.pallas_call(
        _route_kernel,
        grid=(T // tn,),
        in_specs=[pl.BlockSpec((N_EXPERTS, tn), lambda i: (0, i))],
        out_specs=[tok, tok, pl.BlockSpec((META_ROWS, tn), lambda i: (0, i)),
                   pl.BlockSpec((N_EXPERTS, LANES), lambda i: (0, 0))],
        out_shape=[
            jax.ShapeDtypeStruct((SUBLANES, T), I32),
            jax.ShapeDtypeStruct((SUBLANES, T), I32),
            jax.ShapeDtypeStruct((META_ROWS, T), F32),
            jax.ShapeDtypeStruct((N_EXPERTS, LANES), F32),
        ],
        scratch_shapes=[pltpu.VMEM((N_EXPERTS, LANES), F32)],
        compiler_params=_cparams(("arbitrary",)),
        name="route",
    )(logits_t)


def _dest_kernel(pst_ref, ids_ref, rank_ref, dest_ref):
    ids = ids_ref[...]
    start = jnp.zeros(ids.shape, I32)
    for e in range(N_EXPERTS):
        start = jnp.where(ids == e, pst_ref[e], start)
    dest = rank_ref[...] + start
    for j in range(dest_ref.shape[0]):
        dest_ref[j] = dest[:, j * TT_ROWS:(j + 1) * TT_ROWS]


def _dest(pstarts, ids, rank):
    T = ids.shape[1]
    tn = TN_ROUTE
    per = tn // TT_ROWS
    tok = pl.BlockSpec((SUBLANES, tn), lambda i, pst: (0, i))
    return pl.pallas_call(
        _dest_kernel,
        grid_spec=pltpu.PrefetchScalarGridSpec(
            num_scalar_prefetch=1,
            grid=(T // tn,),
            in_specs=[tok, tok],
            out_specs=pl.BlockSpec((per, SUBLANES, TT_ROWS), lambda i, pst: (i, 0, 0)),
        ),
        out_shape=jax.ShapeDtypeStruct((T // TT_ROWS, SUBLANES, TT_ROWS), I32),
        compiler_params=_cparams(("arbitrary",)),
        name="dest",
    )(pstarts, ids, rank)


def _rows_to_tiles(cols):
    return jnp.swapaxes(jnp.stack(cols, axis=0), 0, 1)


def _tiles_to_rows(tiles):
    return jnp.swapaxes(tiles, 0, 1)


def _dispatch_kernel(cnt_ref, pst_ref, pad_ref, nb_ref, dest_hbm, x_ref, meta_ref, xs_hbm,
                     dsm, ext, zbuf, sem_d, sem_x, sem_z):
    tt = x_ref.shape[0]
    i = pl.program_id(0)
    n = pl.num_programs(0)
    slot = i % 2
    n_blocks = xs_hbm.shape[0] // TM_MOE

    def dest_copy(j, sl):
        return pltpu.make_async_copy(dest_hbm.at[j], dsm.at[sl], sem_d.at[sl])

    def zero_copy(r):
        return pltpu.make_async_copy(zbuf.at[0], xs_hbm.at[r], sem_z)

    def zero_block_copy(b):
        r0 = pl.multiple_of(b * TM_MOE, TM_MOE)
        return pltpu.make_async_copy(zbuf, xs_hbm.at[pl.ds(r0, TM_MOE)], sem_z)

    def rows_wait(sl):
        for _ in range(TOP_K):
            pltpu.make_async_copy(ext.at[sl], xs_hbm.at[pl.ds(0, tt)], sem_x.at[sl]).wait()

    def for_each_pad_row(fn):
        def per_expert(e, carry):
            lo = pst_ref[e] + cnt_ref[e]
            hi = pst_ref[e] + pad_ref[e]

            def per_row(r, c2):
                fn(r)
                return c2

            return lax.fori_loop(lo, hi, per_row, carry)

        lax.fori_loop(0, N_EXPERTS, per_expert, 0)

    def for_each_unused_block(fn):
        def per_block(b, carry):
            fn(b)
            return carry

        lax.fori_loop(nb_ref[0], n_blocks, per_block, 0)

    @pl.when(i == 0)
    def _():
        dest_copy(0, 0).start()
        zbuf[...] = jnp.zeros_like(zbuf)
        for_each_pad_row(lambda r: zero_copy(r).start())
        for_each_unused_block(lambda b: zero_block_copy(b).start())

    x = x_ref[...].astype(F32)
    cols = [x[:, j * LANES:(j + 1) * LANES] for j in range(D_MODEL // LANES)]
    meta_pad = jnp.concatenate([meta_ref[...], jnp.zeros((LANES - META_ROWS, tt), F32)], axis=0)
    cols.append(meta_pad.T)
    cols += [jnp.zeros((tt, LANES), F32)] * (ROW_SUBLANES - len(cols))
    ext[slot] = _rows_to_tiles(cols).astype(BF16)

    dest_copy(i, slot).wait()

    @pl.when(i + 1 < n)
    def _():
        dest_copy(i + 1, 1 - slot).start()

    group = 16
    for t0 in range(0, tt, group):
        dsts = [[dsm[slot, k, t] for k in range(TOP_K)] for t in range(t0, t0 + group)]
        for j, t in enumerate(range(t0, t0 + group)):
            for k in range(TOP_K):
                pltpu.make_async_copy(ext.at[slot, t], xs_hbm.at[dsts[j][k]],
                                      sem_x.at[slot]).start(priority=k % 2)

    @pl.when(i == 0)
    def _():
        for_each_pad_row(lambda r: zero_copy(r).wait())
        for_each_unused_block(lambda b: zero_block_copy(b).wait())

    @pl.when(i > 0)
    def _():
        rows_wait(1 - slot)

    @pl.when(i == n - 1)
    def _():
        rows_wait(slot)


def _dispatch(counts, pstarts, padded, nb_used, dest_tiles, xn, meta, n_rows):
    T = xn.shape[0]
    tt = TT_ROWS
    return pl.pallas_call(
        _dispatch_kernel,
        grid_spec=pltpu.PrefetchScalarGridSpec(
            num_scalar_prefetch=4,
            grid=(T // tt,),
            in_specs=[
                pl.BlockSpec(memory_space=pl.ANY),
                pl.BlockSpec((tt, D_MODEL), lambda i, *_: (i, 0)),
                pl.BlockSpec((META_ROWS, tt), lambda i, *_: (0, i)),
            ],
            out_specs=pl.BlockSpec(memory_space=pl.ANY),
            scratch_shapes=[
                pltpu.SMEM((2, SUBLANES, tt), I32),
                pltpu.VMEM((2, tt, ROW_SUBLANES, LANES), BF16),
                pltpu.VMEM((TM_MOE, ROW_SUBLANES, LANES), BF16),
                pltpu.SemaphoreType.DMA((2,)),
                pltpu.SemaphoreType.DMA((2,)),
                pltpu.SemaphoreType.DMA,
            ],
        ),
        out_shape=jax.ShapeDtypeStruct((n_rows, ROW_SUBLANES, LANES), BF16),
        compiler_params=_cparams(("arbitrary",)),
        name="dispatch",
    )(counts, pstarts, padded, nb_used, dest_tiles, xn, meta)


def _experts_kernel(be_ref, nb_ref, xs_ref, wgu_ref, bgu_ref, wd_ref, bd_ref, out_hbm,
                    wgu_bf, wd_bf, x_bf, acc_ref, ybuf, code_v, code_s, sem_c, sem_y):
    tm = TM_MOE
    b = pl.program_id(0)
    slot = b % 2
    nbu = nb_ref[0]
    n_tok = (out_hbm.shape[0] - 2 * tm) // TOP_K
    dump0 = TOP_K * n_tok

    def scatter_wait(sl):
        pltpu.make_async_copy(ybuf.at[sl], out_hbm.at[pl.ds(0, tm)], sem_y.at[sl]).wait()

    def scatter_row(sl, r, dst):
        return pltpu.make_async_copy(ybuf.at[sl, r], out_hbm.at[dst], sem_y.at[sl])

    def code_copy(sl):
        return pltpu.make_async_copy(code_v, code_s.at[sl], sem_c)

    @pl.when(b == 0)
    def _():
        ybuf[...] = jnp.zeros_like(ybuf)
        for sl in range(2):
            cp = pltpu.make_async_copy(ybuf.at[sl], out_hbm.at[pl.ds(dump0 + sl * tm, tm)],
                                       sem_y.at[sl])
            cp.start()
            cp.wait()

        def init_code(r, carry):
            code_s[1, 0, r] = dump0 + tm + r
            return carry

        lax.fori_loop(0, tm, init_code, 0)

    @pl.when(b < nbu)
    def _():
        e = be_ref[b]
        prev = be_ref[jnp.maximum(b - 1, 0)]

        @pl.when((b == 0) | (e != prev))
        def _():
            wgu_bf[...] = wgu_ref[0].astype(BF16)
            wd_bf[...] = wd_ref[0].astype(BF16)

        group = 16
        for r0 in range(0, tm, group):
            dsts = [code_s[1 - slot, 0, r] for r in range(r0, r0 + group)]
            for j, r in enumerate(range(r0, r0 + group)):
                scatter_row(1 - slot, r, dsts[j]).start(priority=r % 2)

        planes = _tiles_to_rows(xs_ref[...].astype(F32))
        meta = planes[D_MODEL // LANES]
        e_f = e.astype(F32)
        tok = meta[:, 0:1] * 256.0 + meta[:, 1:2]
        valid = meta[:, 2 + 3 * TOP_K:3 + 3 * TOP_K] > 0.5
        choice = jnp.zeros((tm, 1), F32)
        gate = jnp.zeros((tm, 1), F32)
        for j in range(TOP_K):
            hit = meta[:, 2 + j:3 + j] == e_f
            choice = jnp.where(hit, float(j), choice)
            gate_j = meta[:, 2 + TOP_K + j:3 + TOP_K + j] + meta[:, 2 + 2 * TOP_K + j:3 + 2 * TOP_K + j]
            gate = jnp.where(hit, gate_j, gate)
        spare = (dump0 + slot * tm + lax.broadcasted_iota(I32, (tm, 1), 0)).astype(F32)
        code = jnp.where(valid, choice * float(n_tok) + tok, spare)
        code_row = jnp.broadcast_to(code, (tm, LANES)).T[0:1, :]
        code_v[...] = code_row.astype(I32)
        code_copy(slot).start()

        x_bf[...] = jnp.concatenate([planes[j].astype(BF16) for j in range(D_MODEL // LANES)],
                                    axis=1)

        for c in range(D_FF // FF_CHUNK):
            gcols = slice(c * FF_CHUNK, (c + 1) * FF_CHUNK)
            ucols = slice(D_FF + c * FF_CHUNK, D_FF + (c + 1) * FF_CHUNK)
            x = x_bf[...]
            g = jnp.dot(x, wgu_bf[:, gcols], preferred_element_type=F32) + bgu_ref[0, :, gcols]
            up = jnp.dot(x, wgu_bf[:, ucols], preferred_element_type=F32) + bgu_ref[0, :, ucols]
            g = jnp.minimum(g, SWIGLU_LIMIT)
            up = jnp.clip(up, -SWIGLU_LIMIT, SWIGLU_LIMIT)
            hdn = ((up + 1.0) * g * jax.nn.sigmoid(g * SWIGLU_ALPHA)).astype(BF16)
            part = jnp.dot(hdn, wd_bf[gcols, :], preferred_element_type=F32)
            if c == 0:
                acc_ref[...] = part
            else:
                acc_ref[...] += part

        @pl.when(b > 0)
        def _():
            scatter_wait(slot)

        y = (acc_ref[...] + bd_ref[0]) * gate
        ybuf[slot] = _rows_to_tiles([y[:, j * LANES:(j + 1) * LANES]
                                     for j in range(D_MODEL // LANES)]).astype(PLANE_DTYPE)
        code_copy(slot).wait()

    @pl.when(b == nbu)
    def _():
        scatter_wait(slot)

        def per_row(r, carry):
            scatter_row(1 - slot, r, code_s[1 - slot, 0, r]).start()
            return carry

        lax.fori_loop(0, tm, per_row, 0, unroll=8)
        scatter_wait(1 - slot)


def _experts(block_e, nb_used, xs, wgu, bgu, wd, bd, n_tok):
    P = xs.shape[0]
    tm = TM_MOE
    nb = P // tm

    def row_block(b, be, nbu):
        return (jnp.minimum(b, nbu[0] - 1), 0, 0)

    def expert_block(b, be, nbu):
        return (be[jnp.minimum(b, nbu[0] - 1)], 0, 0)

    return pl.pallas_call(
        _experts_kernel,
        grid_spec=pltpu.PrefetchScalarGridSpec(
            num_scalar_prefetch=2,
            grid=(nb,),
            in_specs=[
                pl.BlockSpec((tm, ROW_SUBLANES, LANES), row_block),
                pl.BlockSpec((1, D_MODEL, 2 * D_FF), expert_block),
                pl.BlockSpec((1, 1, 2 * D_FF), expert_block),
                pl.BlockSpec((1, D_FF, D_MODEL), expert_block),
                pl.BlockSpec((1, 1, D_MODEL), expert_block),
            ],
            out_specs=pl.BlockSpec(memory_space=pl.ANY),
            scratch_shapes=[
                pltpu.VMEM((D_MODEL, 2 * D_FF), BF16),
                pltpu.VMEM((D_FF, D_MODEL), BF16),
                pltpu.VMEM((tm, D_MODEL), BF16),
                pltpu.VMEM((tm, D_MODEL), F32),
                pltpu.VMEM((2, tm, SUBLANES, LANES), PLANE_DTYPE),
                pltpu.VMEM((1, tm), I32),
                pltpu.SMEM((2, 1, tm), I32),
                pltpu.SemaphoreType.DMA,
                pltpu.SemaphoreType.DMA((2,)),
            ],
        ),
        out_shape=jax.ShapeDtypeStruct((TOP_K * n_tok + 2 * tm, SUBLANES, LANES), PLANE_DTYPE),
        compiler_params=_cparams(("arbitrary",)),
        name="experts",
    )(block_e, nb_used, xs, wgu, bgu, wd, bd)


def _combine_kernel(h1_ref, p0_ref, p1_ref, p2_ref, p3_ref, fw_ref, out_ref):
    moe_tiles = ((p0_ref[...].astype(F32) + p1_ref[...].astype(F32))
                 + (p2_ref[...].astype(F32) + p3_ref[...].astype(F32)))
    moe_cols = _tiles_to_rows(moe_tiles)
    moe = jnp.concatenate([moe_cols[j] for j in range(D_MODEL // LANES)], axis=1)
    out_ref[...] = _rms(h1_ref[...] + moe, fw_ref[...])


def _combine(h1, planes, fw):
    T = h1.shape[0]
    tt = TT_COMBINE
    nt = T // tt

    def plane(k):
        return pl.BlockSpec((tt, SUBLANES, LANES), lambda i, k=k: (k * nt + i, 0, 0))

    return pl.pallas_call(
        _combine_kernel,
        grid=(nt,),
        in_specs=[pl.BlockSpec((tt, D_MODEL), lambda i: (i, 0)),
                  plane(0), plane(1), plane(2), plane(3),
                  pl.BlockSpec((1, D_MODEL), lambda i: (0, 0))],
        out_specs=pl.BlockSpec((tt, D_MODEL), lambda i: (i, 0)),
        out_shape=jax.ShapeDtypeStruct((T, D_MODEL), F32),
        compiler_params=_cparams(("arbitrary",)),
        name="combine",
    )(h1, planes, planes, planes, planes, fw)


def kernel(x, norm1_w, w_in, mlstm_gate_b, mlstm_norm_w, conv_w, conv_b, lru_wr, lru_br, lru_wi,
           lru_bi, lru_lambda, lru_norm_w, w_out, norm2_w, router_w, router_b, moe_w_gu, moe_b_gu,
           moe_w_down, moe_b_down, final_norm_w):
    B, S, D = x.shape
    assert D == D_MODEL and norm1_w.shape[0] == 1
    assert S % TS_SEQ == 0 and (B * S) % TN_ROUTE == 0
    T = B * S
    x2 = x.reshape(T, D)

    w = w_in[0]
    k_off = M_WIDTH
    v_off = 2 * M_WIDTH
    g_off = 3 * M_WIDTH
    o_off = g_off + 2 * M_HEADS
    w_main = jnp.concatenate([w[:, :k_off], w[:, v_off:g_off], w[:, o_off:]], axis=1).astype(BF16)
    w_kt = w[:, k_off:v_off].T.astype(BF16)
    w_gate = jnp.zeros((D, LANES), F32)
    w_gate = w_gate.at[:, 0:M_HEADS].set(w[:, g_off:g_off + M_HEADS])
    w_gate = w_gate.at[:, SUBLANES:SUBLANES + M_HEADS].set(w[:, g_off + M_HEADS:o_off])
    gate_b = jnp.zeros((LANES, 1), F32)
    gate_b = gate_b.at[0:M_HEADS, 0].set(mlstm_gate_b[0, :M_HEADS])
    gate_b = gate_b.at[SUBLANES:SUBLANES + M_HEADS, 0].set(mlstm_gate_b[0, M_HEADS:])
    wri = (0.5 * jnp.concatenate([lru_wr[0], lru_wi[0]], axis=-1)).astype(BF16)
    bri = 0.5 * jnp.concatenate([lru_br[0].reshape(R_HEADS, 1, R_BLOCK),
                                 lru_bi[0].reshape(R_HEADS, 1, R_BLOCK)], axis=-1)
    rw = jnp.zeros((D, LANES), F32).at[:, :N_EXPERTS].set(router_w[0])
    rwh = rw.astype(BF16)
    rwl = (rw - rwh.astype(F32)).astype(BF16)
    rw2 = jnp.concatenate([rwh, rwl], axis=1)

    proj, kt, g_row, rt = _inproj(x2, norm1_w, w_main, w_kt, w_gate.astype(BF16), gate_b)
    h_m = _mlstm(proj, kt, g_row, rt, mlstm_norm_w, B, S)
    h_r = _rglru(proj, conv_w[0], conv_b, wri, bri, lru_lambda, lru_norm_w, B, S)
    h1, xn2, logits_t = _outproj(h_m, h_r, x2, w_out[0].astype(BF16), norm2_w, rw2,
                                 router_b[0].reshape(N_EXPERTS, 1))

    ids, rank, meta, cnt = _route(logits_t)
    counts = cnt[:, 0].astype(I32)
    padded = ((counts + TM_MOE - 1) // TM_MOE) * TM_MOE
    pends = jnp.cumsum(padded)
    pstarts = pends - padded
    nb = (T * TOP_K) // TM_MOE + N_EXPERTS
    nb_used = (pends[-1:] // TM_MOE).astype(I32)
    block_start = jnp.arange(nb, dtype=I32) * TM_MOE
    block_e = jnp.minimum(jnp.sum((pends[None, :] <= block_start[:, None]).astype(I32), axis=1),
                          N_EXPERTS - 1)

    dest_tiles = _dest(pstarts, ids, rank)
    xs = _dispatch(counts, pstarts, padded, nb_used, dest_tiles, xn2, meta, nb * TM_MOE)
    planes = _experts(block_e, nb_used, xs, moe_w_gu[0],
                      moe_b_gu[0].reshape(N_EXPERTS, 1, 2 * D_FF),
                      moe_w_down[0], moe_b_down[0].reshape(N_EXPERTS, 1, D), T)
    out = _combine(h1, planes, final_norm_w.reshape(1, D))
    return out.reshape(B, S, D)
```

```python
import jax
import jax.numpy as jnp
from jax import lax
from jax.experimental import pallas as pl
from jax.experimental.pallas import tpu as pltpu

F32 = jnp.float32
BF16 = jnp.bfloat16
I32 = jnp.int32

D_MODEL = 1024
M_HEADS = 4
HEAD_DIM = 128
M_WIDTH = M_HEADS * HEAD_DIM
R_HEADS = 4
R_BLOCK = 128
R_WIDTH = R_HEADS * R_BLOCK
CONV_WIDTH = 4
LRU_C = 8.0
N_EXPERTS = 32
TOP_K = 4
D_FF = 1024
SWIGLU_LIMIT = 7.0
SWIGLU_ALPHA = 1.702
EPS = 1e-6

N_MAIN = 5 * 512
LANES = 128
SUBLANES = 8
VMEM_LIMIT = 56 * 1024 * 1024

TM_PROJ = 1024
TS_SEQ = 512
CHUNK = 128
TN_ROUTE = 2048
TT_ROWS = 512
TT_COMBINE = 512
TM_MOE = 512
FF_CHUNK = 512
META_ROWS = 16
ROW_SUBLANES = 16
PLANE_DTYPE = jnp.bfloat16
NEG_BIG = -1e30


def _cparams(sem, vmem=VMEM_LIMIT):
    return pltpu.CompilerParams(dimension_semantics=sem, vmem_limit_bytes=vmem)


def _rms(x, w):
    ms = jnp.mean(x * x, axis=-1, keepdims=True)
    return x * lax.rsqrt(ms + EPS) * w


def _sigmoid(x):
    return 0.5 * jnp.tanh(0.5 * x) + 0.5


def _segmented_scan(x, op, ident, lane_in_chunk):
    d = 1
    while d < CHUNK:
        shifted = pltpu.roll(x, d, axis=1)
        x = op(x, jnp.where(lane_in_chunk >= d, shifted, ident))
        d *= 2
    return x


def _inproj_kernel(x_ref, nw_ref, w_ref, wkt_ref, wg_ref, gb_ref,
                   proj_ref, kt_ref, grow_ref, rt_ref):
    tm = x_ref.shape[0]
    xn = _rms(x_ref[...], nw_ref[...]).astype(BF16)

    g = jnp.dot(xn, wg_ref[...], preferred_element_type=F32)
    gt = g.T + gb_ref[...]
    ig = gt[0:SUBLANES]
    fg = gt[SUBLANES:2 * SUBLANES]
    lf = jnp.minimum(fg, 0.0) - jnp.log1p(jnp.exp(-jnp.abs(fg)))
    lane_in_chunk = lax.broadcasted_iota(I32, (SUBLANES, tm), 1) % CHUNK
    b_row = _segmented_scan(lf, jnp.add, 0.0, lane_in_chunk)
    g_row = ig - b_row
    cm_row = _segmented_scan(g_row, jnp.maximum, NEG_BIG, lane_in_chunk)
    grow_ref[...] = g_row
    stacked = jnp.concatenate(
        [b_row, g_row, cm_row, jnp.zeros((LANES - 3 * SUBLANES, tm), F32)], axis=0)
    rt_ref[...] = stacked.T

    for c in range(N_MAIN // 512):
        cols = slice(c * 512, (c + 1) * 512)
        proj_ref[:, cols] = jnp.dot(xn, w_ref[:, cols], preferred_element_type=F32).astype(BF16)

    kt_ref[...] = lax.dot_general(wkt_ref[...], xn, (((1,), (1,)), ((), ())),
                                  preferred_element_type=F32).astype(BF16)


def _inproj(x2, nw, w_main, w_kt, w_gate, gate_b):
    T = x2.shape[0]
    tm = TM_PROJ

    def const(shape):
        return pl.BlockSpec(shape, lambda i: (0,) * len(shape))

    return pl.pallas_call(
        _inproj_kernel,
        grid=(T // tm,),
        in_specs=[
            pl.BlockSpec((tm, D_MODEL), lambda i: (i, 0)),
            const((1, D_MODEL)), const((D_MODEL, N_MAIN)), const((M_WIDTH, D_MODEL)),
            const((D_MODEL, LANES)), const((LANES, 1)),
        ],
        out_specs=[
            pl.BlockSpec((tm, N_MAIN), lambda i: (i, 0)),
            pl.BlockSpec((M_WIDTH, tm), lambda i: (0, i)),
            pl.BlockSpec((SUBLANES, tm), lambda i: (0, i)),
            pl.BlockSpec((tm, LANES), lambda i: (i, 0)),
        ],
        out_shape=[
            jax.ShapeDtypeStruct((T, N_MAIN), BF16),
            jax.ShapeDtypeStruct((M_WIDTH, T), BF16),
            jax.ShapeDtypeStruct((SUBLANES, T), F32),
            jax.ShapeDtypeStruct((T, LANES), F32),
        ],
        compiler_params=_cparams(("arbitrary",)),
        name="inproj",
    )(x2, nw, w_main, w_kt, w_gate, gate_b)


def _mlstm_kernel(q_ref, v_ref, o_ref, kt_ref, grow_ref, rt_ref, nw_ref, out_ref,
                  c_ref, m_ref, bc_ref):
    ts = q_ref.shape[0]
    scale = HEAD_DIM ** -0.5

    @pl.when(pl.program_id(1) == 0)
    def _():
        c_ref[...] = jnp.zeros_like(c_ref)
        m_ref[...] = jnp.zeros_like(m_ref)

    for idx, term in enumerate((0, 2)):
        for h in range(M_HEADS):
            col = term * SUBLANES + h
            bc_ref[idx, h] = jnp.broadcast_to(rt_ref[:, col:col + 1], (ts, LANES))

    t_idx = lax.broadcasted_iota(I32, (CHUNK, CHUNK), 0)
    s_idx = lax.broadcasted_iota(I32, (CHUNK, CHUNK), 1)
    causal = t_idx >= s_idx
    ones_blk = jnp.ones((CHUNK, HEAD_DIM), BF16)

    for c in range(ts // CHUNK):
        rows = slice(c * CHUNK, (c + 1) * CHUNK)
        last = slice((c + 1) * CHUNK - 1, (c + 1) * CHUNK)
        for h in range(M_HEADS):
            cols = slice(h * HEAD_DIM, (h + 1) * HEAD_DIM)
            qc = q_ref[rows, cols]
            kt = kt_ref[cols, rows]
            v_ext = jnp.concatenate([v_ref[rows, cols], ones_blk], axis=1)
            b_bc = bc_ref[0, h, rows, :]
            cm_bc = bc_ref[1, h, rows, :]
            g_r = grow_ref[h:h + 1, rows]
            m_prev = m_ref[h:h + 1, :]
            c_prev = c_ref[h]

            mx = jnp.maximum(cm_bc, m_prev)
            decay_mat = jnp.exp(jnp.where(causal, g_r - mx, NEG_BIG))
            scores = jnp.dot(qc, kt, preferred_element_type=F32)
            p = (scores * decay_mat * scale).astype(BF16)
            intra = jnp.dot(p, v_ext, preferred_element_type=F32)
            inter = jnp.dot(qc, c_prev.astype(BF16), preferred_element_type=F32)
            w_inter = jnp.exp(m_prev - mx)
            num = intra[:, :HEAD_DIM] + w_inter * inter[:, :HEAD_DIM]
            den = intra[:, HEAD_DIM:] + w_inter * inter[:, HEAD_DIM:]
            m_t = b_bc + mx
            h_t = num / jnp.maximum(jnp.abs(den), jnp.exp(-m_t))

            mx_last = jnp.maximum(bc_ref[1, h, last, :], m_prev)
            w_row = jnp.exp(g_r - mx_last) * scale
            kw_t = (kt.astype(F32) * w_row).astype(BF16)
            upd = jnp.dot(kw_t, v_ext, preferred_element_type=F32)
            dec = jnp.exp(m_prev - mx_last)
            c_ref[h] = jnp.concatenate([dec, dec], axis=1) * c_prev + upd
            m_ref[h:h + 1, :] = bc_ref[0, h, last, :] + mx_last

            y = _sigmoid(o_ref[rows, cols].astype(F32)) * h_t
            y2 = y * y
            y2_hi = y2.astype(BF16)
            y2_lo = (y2 - y2_hi.astype(F32)).astype(BF16)
            ss = (jnp.dot(y2_hi, ones_blk, preferred_element_type=F32)
                  + jnp.dot(y2_lo, ones_blk, preferred_element_type=F32))
            out_ref[rows, cols] = (y * lax.rsqrt(ss * (1.0 / HEAD_DIM) + EPS)
                                   * nw_ref[:, cols]).astype(BF16)


def _mlstm(proj, kt, g_row, rt, nw, B, S):
    T = B * S
    ts = TS_SEQ
    nst = S // ts

    def col(j):
        return pl.BlockSpec((ts, M_WIDTH), lambda b, s, j=j: (b * nst + s, j))

    return pl.pallas_call(
        _mlstm_kernel,
        grid=(B, nst),
        in_specs=[col(0), col(1), col(2),
                  pl.BlockSpec((M_WIDTH, ts), lambda b, s: (0, b * nst + s)),
                  pl.BlockSpec((SUBLANES, ts), lambda b, s: (0, b * nst + s)),
                  pl.BlockSpec((ts, LANES), lambda b, s: (b * nst + s, 0)),
                  pl.BlockSpec((1, M_WIDTH), lambda b, s: (0, 0))],
        out_specs=pl.BlockSpec((ts, M_WIDTH), lambda b, s: (b * nst + s, 0)),
        out_shape=jax.ShapeDtypeStruct((T, M_WIDTH), BF16),
        scratch_shapes=[
            pltpu.VMEM((M_HEADS, HEAD_DIM, 2 * HEAD_DIM), F32),
            pltpu.VMEM((SUBLANES, LANES), F32),
            pltpu.VMEM((2, M_HEADS, ts, LANES), F32),
        ],
        compiler_params=_cparams(("arbitrary", "arbitrary")),
        name="mlstm",
    )(proj, proj, proj, kt, g_row, rt, nw)


def _gelu_tanh(x):
    return 0.5 * x * (1.0 + jnp.tanh(0.7978845608028654 * (x + 0.044715 * (x * x * x))))


def _rglru_kernel(rx_ref, rg_ref, cw_ref, cb_ref, wri_ref, bri_ref, lam_ref, nw_ref, out_ref,
                  xpad_ref, a_ref, b_ref, hcar_ref):
    ts = rx_ref.shape[0]
    ng = ts // SUBLANES

    @pl.when(pl.program_id(1) == 0)
    def _():
        xpad_ref[0:SUBLANES, :] = jnp.zeros((SUBLANES, R_WIDTH), F32)
        hcar_ref[...] = jnp.zeros_like(hcar_ref)

    x = rx_ref[...].astype(F32)
    xpad_ref[SUBLANES:SUBLANES + ts, :] = x
    xp = xpad_ref[...]
    xc = cb_ref[...]
    for j in range(CONV_WIDTH):
        sh = CONV_WIDTH - 1 - j
        tap = xp if sh == 0 else pltpu.roll(xp, sh, axis=0)
        xc = xc + cw_ref[j:j + 1, :] * tap[SUBLANES:SUBLANES + ts, :]
    xpad_ref[0:SUBLANES, :] = x[ts - SUBLANES:ts, :]

    nlam = -lam_ref[...]
    softplus_nlam = jnp.maximum(nlam, 0.0) + jnp.log1p(jnp.exp(-jnp.abs(nlam)))
    row_in_group = lax.broadcasted_iota(I32, (ng, SUBLANES, R_BLOCK), 1)

    for h in range(R_HEADS):
        cols = slice(h * R_BLOCK, (h + 1) * R_BLOCK)
        xh = xc[:, cols]
        half_z = jnp.dot(xh.astype(BF16), wri_ref[h], preferred_element_type=F32) + bri_ref[h]
        t_r = jnp.tanh(half_z[:, :R_BLOCK])
        t_i = jnp.tanh(half_z[:, R_BLOCK:])
        c_half = (-0.5 * LRU_C) * softplus_nlam[:, cols]
        log_a = c_half * t_r + c_half
        a = jnp.exp(log_a)
        one_m_a2 = -jnp.tanh(log_a) * (a * a + 1.0)
        half_x = 0.5 * xh
        bt = jnp.sqrt(one_m_a2) * (half_x * t_i + half_x)
        a = a.reshape(ng, SUBLANES, R_BLOCK)
        bt = bt.reshape(ng, SUBLANES, R_BLOCK)
        d = 1
        while d < SUBLANES:
            keep = row_in_group >= d
            a_sh = pltpu.roll(a, d, axis=1)
            b_sh = pltpu.roll(bt, d, axis=1)
            bt = bt + jnp.where(keep, a * b_sh, 0.0)
            a = a * jnp.where(keep, a_sh, 1.0)
            d *= 2
        a_ref[:, cols] = a.reshape(ts, R_BLOCK)
        b_ref[:, cols] = bt.reshape(ts, R_BLOCK)

    def group_step(g, hprev):
        r0 = pl.multiple_of(g * SUBLANES, SUBLANES)
        hr = b_ref[pl.ds(r0, SUBLANES), :] + a_ref[pl.ds(r0, SUBLANES), :] * hprev
        b_ref[pl.ds(r0, SUBLANES), :] = hr
        return hr[SUBLANES - 1:SUBLANES, :]

    hcar_ref[...] = lax.fori_loop(0, ng, group_step, hcar_ref[...], unroll=4)

    y = _gelu_tanh(rg_ref[...].astype(F32)) * b_ref[...]
    out_ref[...] = _rms(y, nw_ref[...]).astype(BF16)


def _rglru(proj, cw, cb, wri, bri, lam, nw, B, S):
    T = B * S
    ts = TS_SEQ
    nst = S // ts

    def col(j):
        return pl.BlockSpec((ts, R_WIDTH), lambda b, s, j=j: (b * nst + s, j))

    def const(shape):
        return pl.BlockSpec(shape, lambda b, s: (0,) * len(shape))

    return pl.pallas_call(
        _rglru_kernel,
        grid=(B, nst),
        in_specs=[col(3), col(4), const((CONV_WIDTH, R_WIDTH)), const((1, R_WIDTH)),
                  const((R_HEADS, R_BLOCK, 2 * R_BLOCK)), const((R_HEADS, 1, 2 * R_BLOCK)),
                  const((1, R_WIDTH)), const((1, R_WIDTH))],
        out_specs=pl.BlockSpec((ts, R_WIDTH), lambda b, s: (b * nst + s, 0)),
        out_shape=jax.ShapeDtypeStruct((T, R_WIDTH), BF16),
        scratch_shapes=[
            pltpu.VMEM((ts + SUBLANES, R_WIDTH), F32),
            pltpu.VMEM((ts, R_WIDTH), F32),
            pltpu.VMEM((ts, R_WIDTH), F32),
            pltpu.VMEM((1, R_WIDTH), F32),
        ],
        compiler_params=_cparams(("arbitrary", "arbitrary")),
        name="rglru",
    )(proj, proj, cw, cb, wri, bri, lam, nw)


def _outproj_kernel(hm_ref, hr_ref, x_ref, wo_ref, n2_ref, rw2_ref, rb_ref,
                    h1_ref, xn_ref, lg_ref):
    mix = jnp.dot(hm_ref[...], wo_ref[0:M_WIDTH, :], preferred_element_type=F32)
    mix = mix + jnp.dot(hr_ref[...], wo_ref[M_WIDTH:, :], preferred_element_type=F32)
    h1 = x_ref[...] + mix
    h1_ref[...] = h1
    xn = _rms(h1, n2_ref[...])
    xh = xn.astype(BF16)
    xn_ref[...] = xh
    xl = (xn - xh.astype(F32)).astype(BF16)
    lg2 = jnp.dot(xh, rw2_ref[...], preferred_element_type=F32)
    lg = lg2[:, :LANES] + lg2[:, LANES:]
    lg = lg + jnp.dot(xl, rw2_ref[:, :LANES], preferred_element_type=F32)
    lg_ref[...] = lg.T[0:N_EXPERTS] + rb_ref[...]


def _outproj(hm, hr, x2, wo, n2, rw2, rb):
    T = x2.shape[0]
    tm = TM_PROJ

    def const(shape):
        return pl.BlockSpec(shape, lambda i: (0,) * len(shape))

    return pl.pallas_call(
        _outproj_kernel,
        grid=(T // tm,),
        in_specs=[
            pl.BlockSpec((tm, M_WIDTH), lambda i: (i, 0)),
            pl.BlockSpec((tm, R_WIDTH), lambda i: (i, 0)),
            pl.BlockSpec((tm, D_MODEL), lambda i: (i, 0)),
            const((D_MODEL, D_MODEL)), const((1, D_MODEL)),
            const((D_MODEL, 2 * LANES)), const((N_EXPERTS, 1)),
        ],
        out_specs=[
            pl.BlockSpec((tm, D_MODEL), lambda i: (i, 0)),
            pl.BlockSpec((tm, D_MODEL), lambda i: (i, 0)),
            pl.BlockSpec((N_EXPERTS, tm), lambda i: (0, i)),
        ],
        out_shape=[
            jax.ShapeDtypeStruct((T, D_MODEL), F32),
            jax.ShapeDtypeStruct((T, D_MODEL), BF16),
            jax.ShapeDtypeStruct((N_EXPERTS, T), F32),
        ],
        compiler_params=_cparams(("arbitrary",)),
        name="outproj",
    )(hm, hr, x2, wo, n2, rw2, rb)


def _route_kernel(lg_ref, ids_ref, rank_ref, meta_ref, cnt_ref, carry_ref):
    tn = lg_ref.shape[1]
    sub = 256

    @pl.when(pl.program_id(0) == 0)
    def _():
        carry_ref[...] = jnp.zeros_like(carry_ref)

    logits = lg_ref[...]
    e_idx = lax.broadcasted_iota(I32, (N_EXPERTS, tn), 0)
    vals, idxs, sels = [], [], []
    for _ in range(TOP_K):
        m = jnp.max(logits, axis=0, keepdims=True)
        idx = jnp.min(jnp.where(logits == m, e_idx, N_EXPERTS), axis=0, keepdims=True)
        sel = e_idx == idx
        logits = jnp.where(sel, -jnp.inf, logits)
        vals.append(m)
        idxs.append(idx)
        sels.append(sel)
    exps = [jnp.exp(v - vals[0]) for v in vals]
    denom = exps[0] + exps[1] + exps[2] + exps[3]
    chosen = (sels[0] | sels[1] | sels[2] | sels[3]).astype(BF16)

    s_i = lax.broadcasted_iota(I32, (sub, sub), 0)
    t_i = lax.broadcasted_iota(I32, (sub, sub), 1)
    upper = (s_i < t_i).astype(BF16)
    carry = carry_ref[:, 0:1]
    parts = []
    for c in range(tn // sub):
        mc = chosen[:, c * sub:(c + 1) * sub]
        cs = jnp.dot(mc, upper, preferred_element_type=F32) + carry
        parts.append(cs)
        carry = cs[:, sub - 1:sub] + mc[:, sub - 1:sub].astype(F32)
    before = jnp.concatenate(parts, axis=1)
    carry_ref[...] = jnp.broadcast_to(carry, carry_ref.shape)
    cnt_ref[...] = jnp.broadcast_to(carry, cnt_ref.shape)

    zeros_i = jnp.zeros((SUBLANES - TOP_K, tn), I32)
    ranks = [jnp.sum(jnp.where(s, before, 0.0), axis=0, keepdims=True).astype(I32) for s in sels]
    ids_ref[...] = jnp.concatenate(idxs + [zeros_i], axis=0)
    rank_ref[...] = jnp.concatenate(ranks + [zeros_i], axis=0)
    tok = pl.program_id(0) * tn + lax.broadcasted_iota(I32, (1, tn), 1)
    gates = [e / denom for e in exps]
    gate_hi = [g.astype(BF16).astype(F32) for g in gates]
    gate_lo = [g - h for g, h in zip(gates, gate_hi)]
    meta_ref[...] = jnp.concatenate(
        [(tok // 256).astype(F32), (tok % 256).astype(F32)]
        + [i.astype(F32) for i in idxs] + gate_hi + gate_lo
        + [jnp.ones((1, tn), F32), jnp.zeros((META_ROWS - 3 - 3 * TOP_K, tn), F32)], axis=0)


def _route(logits_t):
    T = logits_t.shape[1]
    tn = TN_ROUTE
    tok = pl.BlockSpec((SUBLANES, tn), lambda i: (0, i))
    return pl.pallas_call(
        _route_kernel,
        grid=(T // tn,),
        in_specs=[pl.BlockSpec((N_EXPERTS, tn), lambda i: (0, i))],
        out_specs=[tok, tok, pl.BlockSpec((META_ROWS, tn), lambda i: (0, i)),
                   pl.BlockSpec((N_EXPERTS, LANES), lambda i: (0, 0))],
        out_shape=[
            jax.ShapeDtypeStruct((SUBLANES, T), I32),
            jax.ShapeDtypeStruct((SUBLANES, T), I32),
            jax.ShapeDtypeStruct((META_ROWS, T), F32),
            jax.ShapeDtypeStruct((N_EXPERTS, LANES), F32),
        ],
        scratch_shapes=[pltpu.VMEM((N_EXPERTS, LANES), F32)],
        compiler_params=_cparams(("arbitrary",)),
        name="route",
    )(logits_t)


def _dest_kernel(pst_ref, ids_ref, rank_ref, dest_ref):
    ids = ids_ref[...]
    start = jnp.zeros(ids.shape, I32)
    for e in range(N_EXPERTS):
        start = jnp.where(ids == e, pst_ref[e], start)
    dest = rank_ref[...] + start
    for j in range(dest_ref.shape[0]):
        dest_ref[j] = dest[:, j * TT_ROWS:(j + 1) * TT_ROWS]


def _dest(pstarts, ids, rank):
    T = ids.shape[1]
    tn = TN_ROUTE
    per = tn // TT_ROWS
    tok = pl.BlockSpec((SUBLANES, tn), lambda i, pst: (0, i))
    return pl.pallas_call(
        _dest_kernel,
        grid_spec=pltpu.PrefetchScalarGridSpec(
            num_scalar_prefetch=1,
            grid=(T // tn,),
            in_specs=[tok, tok],
            out_specs=pl.BlockSpec((per, SUBLANES, TT_ROWS), lambda i, pst: (i, 0, 0)),
        ),
        out_shape=jax.ShapeDtypeStruct((T // TT_ROWS, SUBLANES, TT_ROWS), I32),
        compiler_params=_cparams(("arbitrary",)),
        name="dest",
    )(pstarts, ids, rank)


def _rows_to_tiles(cols):
    return jnp.swapaxes(jnp.stack(cols, axis=0), 0, 1)


def _tiles_to_rows(tiles):
    return jnp.swapaxes(tiles, 0, 1)


def _dispatch_kernel(cnt_ref, pst_ref, pad_ref, nb_ref, dest_hbm, x_ref, meta_ref, xs_hbm,
                     dsm, ext, zbuf, sem_d, sem_x, sem_z):
    tt = x_ref.shape[0]
    i = pl.program_id(0)
    n = pl.num_programs(0)
    slot = i % 2
    n_blocks = xs_hbm.shape[0] // TM_MOE

    def dest_copy(j, sl):
        return pltpu.make_async_copy(dest_hbm.at[j], dsm.at[sl], sem_d.at[sl])

    def zero_copy(r):
        return pltpu.make_async_copy(zbuf.at[0], xs_hbm.at[r], sem_z)

    def zero_block_copy(b):
        r0 = pl.multiple_of(b * TM_MOE, TM_MOE)
        return pltpu.make_async_copy(zbuf, xs_hbm.at[pl.ds(r0, TM_MOE)], sem_z)

    def rows_wait(sl):
        for _ in range(TOP_K):
            pltpu.make_async_copy(ext.at[sl], xs_hbm.at[pl.ds(0, tt)], sem_x.at[sl]).wait()

    def for_each_pad_row(fn):
        def per_expert(e, carry):
            lo = pst_ref[e] + cnt_ref[e]
            hi = pst_ref[e] + pad_ref[e]

            def per_row(r, c2):
                fn(r)
                return c2

            return lax.fori_loop(lo, hi, per_row, carry)

        lax.fori_loop(0, N_EXPERTS, per_expert, 0)

    def for_each_unused_block(fn):
        def per_block(b, carry):
            fn(b)
            return carry

        lax.fori_loop(nb_ref[0], n_blocks, per_block, 0)

    @pl.when(i == 0)
    def _():
        dest_copy(0, 0).start()
        zbuf[...] = jnp.zeros_like(zbuf)
        for_each_pad_row(lambda r: zero_copy(r).start())
        for_each_unused_block(lambda b: zero_block_copy(b).start())

    x = x_ref[...].astype(F32)
    cols = [x[:, j * LANES:(j + 1) * LANES] for j in range(D_MODEL // LANES)]
    meta_pad = jnp.concatenate([meta_ref[...], jnp.zeros((LANES - META_ROWS, tt), F32)], axis=0)
    cols.append(meta_pad.T)
    cols += [jnp.zeros((tt, LANES), F32)] * (ROW_SUBLANES - len(cols))
    ext[slot] = _rows_to_tiles(cols).astype(BF16)

    dest_copy(i, slot).wait()

    @pl.when(i + 1 < n)
    def _():
        dest_copy(i + 1, 1 - slot).start()

    group = 16
    for t0 in range(0, tt, group):
        dsts = [[dsm[slot, k, t] for k in range(TOP_K)] for t in range(t0, t0 + group)]
        for j, t in enumerate(range(t0, t0 + group)):
            for k in range(TOP_K):
                pltpu.make_async_copy(ext.at[slot, t], xs_hbm.at[dsts[j][k]],
                                      sem_x.at[slot]).start(priority=k % 2)

    @pl.when(i == 0)
    def _():
        for_each_pad_row(lambda r: zero_copy(r).wait())
        for_each_unused_block(lambda b: zero_block_copy(b).wait())

    @pl.when(i > 0)
    def _():
        rows_wait(1 - slot)

    @pl.when(i == n - 1)
    def _():
        rows_wait(slot)


def _dispatch(counts, pstarts, padded, nb_used, dest_tiles, xn, meta, n_rows):
    T = xn.shape[0]
    tt = TT_ROWS
    return pl.pallas_call(
        _dispatch_kernel,
        grid_spec=pltpu.PrefetchScalarGridSpec(
            num_scalar_prefetch=4,
            grid=(T // tt,),
            in_specs=[
                pl.BlockSpec(memory_space=pl.ANY),
                pl.BlockSpec((tt, D_MODEL), lambda i, *_: (i, 0)),
                pl.BlockSpec((META_ROWS, tt), lambda i, *_: (0, i)),
            ],
            out_specs=pl.BlockSpec(memory_space=pl.ANY),
            scratch_shapes=[
                pltpu.SMEM((2, SUBLANES, tt), I32),
                pltpu.VMEM((2, tt, ROW_SUBLANES, LANES), BF16),
                pltpu.VMEM((TM_MOE, ROW_SUBLANES, LANES), BF16),
                pltpu.SemaphoreType.DMA((2,)),
                pltpu.SemaphoreType.DMA((2,)),
                pltpu.SemaphoreType.DMA,
            ],
        ),
        out_shape=jax.ShapeDtypeStruct((n_rows, ROW_SUBLANES, LANES), BF16),
        compiler_params=_cparams(("arbitrary",)),
        name="dispatch",
    )(counts, pstarts, padded, nb_used, dest_tiles, xn, meta)


def _experts_kernel(be_ref, nb_ref, xs_ref, wgu_ref, bgu_ref, wd_ref, bd_ref, out_hbm,
                    wgu_bf, wd_bf, x_bf, acc_ref, ybuf, code_v, code_s, sem_c, sem_y):
    tm = TM_MOE
    b = pl.program_id(0)
    slot = b % 2
    nbu = nb_ref[0]
    n_tok = (out_hbm.shape[0] - 2 * tm) // TOP_K
    dump0 = TOP_K * n_tok

    def scatter_wait(sl):
        pltpu.make_async_copy(ybuf.at[sl], out_hbm.at[pl.ds(0, tm)], sem_y.at[sl]).wait()

    def scatter_row(sl, r, dst):
        return pltpu.make_async_copy(ybuf.at[sl, r], out_hbm.at[dst], sem_y.at[sl])

    def code_copy(sl):
        return pltpu.make_async_copy(code_v, code_s.at[sl], sem_c)

    @pl.when(b == 0)
    def _():
        ybuf[...] = jnp.zeros_like(ybuf)
        for sl in range(2):
            cp = pltpu.make_async_copy(ybuf.at[sl], out_hbm.at[pl.ds(dump0 + sl * tm, tm)],
                                       sem_y.at[sl])
            cp.start()
            cp.wait()

        def init_code(r, carry):
            code_s[1, 0, r] = dump0 + tm + r
            return carry

        lax.fori_loop(0, tm, init_code, 0)

    @pl.when(b < nbu)
    def _():
        e = be_ref[b]
        prev = be_ref[jnp.maximum(b - 1, 0)]

        @pl.when((b == 0) | (e != prev))
        def _():
            wgu_bf[...] = wgu_ref[0].astype(BF16)
            wd_bf[...] = wd_ref[0].astype(BF16)

        group = 16
        for r0 in range(0, tm, group):
            dsts = [code_s[1 - slot, 0, r] for r in range(r0, r0 + group)]
            for j, r in enumerate(range(r0, r0 + group)):
                scatter_row(1 - slot, r, dsts[j]).start(priority=r % 2)

        planes = _tiles_to_rows(xs_ref[...].astype(F32))
        meta = planes[D_MODEL // LANES]
        e_f = e.astype(F32)
        tok = meta[:, 0:1] * 256.0 + meta[:, 1:2]
        valid = meta[:, 2 + 3 * TOP_K:3 + 3 * TOP_K] > 0.5
        choice = jnp.zeros((tm, 1), F32)
        gate = jnp.zeros((tm, 1), F32)
        for j in range(TOP_K):
            hit = meta[:, 2 + j:3 + j] == e_f
            choice = jnp.where(hit, float(j), choice)
            gate_j = meta[:, 2 + TOP_K + j:3 + TOP_K + j] + meta[:, 2 + 2 * TOP_K + j:3 + 2 * TOP_K + j]
            gate = jnp.where(hit, gate_j, gate)
        spare = (dump0 + slot * tm + lax.broadcasted_iota(I32, (tm, 1), 0)).astype(F32)
        code = jnp.where(valid, choice * float(n_tok) + tok, spare)
        code_row = jnp.broadcast_to(code, (tm, LANES)).T[0:1, :]
        code_v[...] = code_row.astype(I32)
        code_copy(slot).start()

        x_bf[...] = jnp.concatenate([planes[j].astype(BF16) for j in range(D_MODEL // LANES)],
                                    axis=1)

        for c in range(D_FF // FF_CHUNK):
            gcols = slice(c * FF_CHUNK, (c + 1) * FF_CHUNK)
            ucols = slice(D_FF + c * FF_CHUNK, D_FF + (c + 1) * FF_CHUNK)
            x = x_bf[...]
            g = jnp.dot(x, wgu_bf[:, gcols], preferred_element_type=F32) + bgu_ref[0, :, gcols]
            up = jnp.dot(x, wgu_bf[:, ucols], preferred_element_type=F32) + bgu_ref[0, :, ucols]
            g = jnp.minimum(g, SWIGLU_LIMIT)
            up = jnp.clip(up, -SWIGLU_LIMIT, SWIGLU_LIMIT)
            hdn = ((up + 1.0) * g * jax.nn.sigmoid(g * SWIGLU_ALPHA)).astype(BF16)
            part = jnp.dot(hdn, wd_bf[gcols, :], preferred_element_type=F32)
            if c == 0:
                acc_ref[...] = part
            else:
                acc_ref[...] += part

        @pl.when(b > 0)
        def _():
            scatter_wait(slot)

        y = (acc_ref[...] + bd_ref[0]) * gate
        ybuf[slot] = _rows_to_tiles([y[:, j * LANES:(j + 1) * LANES]
                                     for j in range(D_MODEL // LANES)]).astype(PLANE_DTYPE)
        code_copy(slot).wait()

    @pl.when(b == nbu)
    def _():
        scatter_wait(slot)

        def per_row(r, carry):
            scatter_row(1 - slot, r, code_s[1 - slot, 0, r]).start()
            return carry

        lax.fori_loop(0, tm, per_row, 0, unroll=8)
        scatter_wait(1 - slot)


def _experts(block_e, nb_used, xs, wgu, bgu, wd, bd, n_tok):
    P = xs.shape[0]
    tm = TM_MOE
    nb = P // tm

    def row_block(b, be, nbu):
        return (jnp.minimum(b, nbu[0] - 1), 0, 0)

    def expert_block(b, be, nbu):
        return (be[jnp.minimum(b, nbu[0] - 1)], 0, 0)

    return pl.pallas_call(
        _experts_kernel,
        grid_spec=pltpu.PrefetchScalarGridSpec(
            num_scalar_prefetch=2,
            grid=(nb,),
            in_specs=[
                pl.BlockSpec((tm, ROW_SUBLANES, LANES), row_block),
                pl.BlockSpec((1, D_MODEL, 2 * D_FF), expert_block),
                pl.BlockSpec((1, 1, 2 * D_FF), expert_block),
                pl.BlockSpec((1, D_FF, D_MODEL), expert_block),
                pl.BlockSpec((1, 1, D_MODEL), expert_block),
            ],
            out_specs=pl.BlockSpec(memory_space=pl.ANY),
            scratch_shapes=[
                pltpu.VMEM((D_MODEL, 2 * D_FF), BF16),
                pltpu.VMEM((D_FF, D_MODEL), BF16),
                pltpu.VMEM((tm, D_MODEL), BF16),
                pltpu.VMEM((tm, D_MODEL), F32),
                pltpu.VMEM((2, tm, SUBLANES, LANES), PLANE_DTYPE),
                pltpu.VMEM((1, tm), I32),
                pltpu.SMEM((2, 1, tm), I32),
                pltpu.SemaphoreType.DMA,
                pltpu.SemaphoreType.DMA((2,)),
            ],
        ),
        out_shape=jax.ShapeDtypeStruct((TOP_K * n_tok + 2 * tm, SUBLANES, LANES), PLANE_DTYPE),
        compiler_params=_cparams(("arbitrary",)),
        name="experts",
    )(block_e, nb_used, xs, wgu, bgu, wd, bd)


def _combine_kernel(h1_ref, p0_ref, p1_ref, p2_ref, p3_ref, fw_ref, out_ref):
    moe_tiles = ((p0_ref[...].astype(F32) + p1_ref[...].astype(F32))
                 + (p2_ref[...].astype(F32) + p3_ref[...].astype(F32)))
    moe_cols = _tiles_to_rows(moe_tiles)
    moe = jnp.concatenate([moe_cols[j] for j in range(D_MODEL // LANES)], axis=1)
    out_ref[...] = _rms(h1_ref[...] + moe, fw_ref[...])


def _combine(h1, planes, fw):
    T = h1.shape[0]
    tt = TT_COMBINE
    nt = T // tt

    def plane(k):
        return pl.BlockSpec((tt, SUBLANES, LANES), lambda i, k=k: (k * nt + i, 0, 0))

    return pl.pallas_call(
        _combine_kernel,
        grid=(nt,),
        in_specs=[pl.BlockSpec((tt, D_MODEL), lambda i: (i, 0)),
                  plane(0), plane(1), plane(2), plane(3),
                  pl.BlockSpec((1, D_MODEL), lambda i: (0, 0))],
        out_specs=pl.BlockSpec((tt, D_MODEL), lambda i: (i, 0)),
        out_shape=jax.ShapeDtypeStruct((T, D_MODEL), F32),
        compiler_params=_cparams(("arbitrary",)),
        name="combine",
    )(h1, planes, planes, planes, planes, fw)


def kernel(x, norm1_w, w_in, mlstm_gate_b, mlstm_norm_w, conv_w, conv_b, lru_wr, lru_br, lru_wi,
           lru_bi, lru_lambda, lru_norm_w, w_out, norm2_w, router_w, router_b, moe_w_gu, moe_b_gu,
           moe_w_down, moe_b_down, final_norm_w):
    B, S, D = x.shape
    assert D == D_MODEL and norm1_w.shape[0] == 1
    assert S % TS_SEQ == 0 and (B * S) % TN_ROUTE == 0
    T = B * S
    x2 = x.reshape(T, D)

    w = w_in[0]
    k_off = M_WIDTH
    v_off = 2 * M_WIDTH
    g_off = 3 * M_WIDTH
    o_off = g_off + 2 * M_HEADS
    w_main = jnp.concatenate([w[:, :k_off], w[:, v_off:g_off], w[:, o_off:]], axis=1).astype(BF16)
    w_kt = w[:, k_off:v_off].T.astype(BF16)
    w_gate = jnp.zeros((D, LANES), F32)
    w_gate = w_gate.at[:, 0:M_HEADS].set(w[:, g_off:g_off + M_HEADS])
    w_gate = w_gate.at[:, SUBLANES:SUBLANES + M_HEADS].set(w[:, g_off + M_HEADS:o_off])
    gate_b = jnp.zeros((LANES, 1), F32)
    gate_b = gate_b.at[0:M_HEADS, 0].set(mlstm_gate_b[0, :M_HEADS])
    gate_b = gate_b.at[SUBLANES:SUBLANES + M_HEADS, 0].set(mlstm_gate_b[0, M_HEADS:])
    wri = (0.5 * jnp.concatenate([lru_wr[0], lru_wi[0]], axis=-1)).astype(BF16)
    bri = 0.5 * jnp.concatenate([lru_br[0].reshape(R_HEADS, 1, R_BLOCK),
                                 lru_bi[0].reshape(R_HEADS, 1, R_BLOCK)], axis=-1)
    rw = jnp.zeros((D, LANES), F32).at[:, :N_EXPERTS].set(router_w[0])
    rwh = rw.astype(BF16)
    rwl = (rw - rwh.astype(F32)).astype(BF16)
    rw2 = jnp.concatenate([rwh, rwl], axis=1)

    proj, kt, g_row, rt = _inproj(x2, norm1_w, w_main, w_kt, w_gate.astype(BF16), gate_b)
    h_m = _mlstm(proj, kt, g_row, rt, mlstm_norm_w, B, S)
    h_r = _rglru(proj, conv_w[0], conv_b, wri, bri, lru_lambda, lru_norm_w, B, S)
    h1, xn2, logits_t = _outproj(h_m, h_r, x2, w_out[0].astype(BF16), norm2_w, rw2,
                                 router_b[0].reshape(N_EXPERTS, 1))

    ids, rank, meta, cnt = _route(logits_t)
    counts = cnt[:, 0].astype(I32)
    padded = ((counts + TM_MOE - 1) // TM_MOE) * TM_MOE
    pends = jnp.cumsum(padded)
    pstarts = pends - padded
    nb = (T * TOP_K) // TM_MOE + N_EXPERTS
    nb_used = (pends[-1:] // TM_MOE).astype(I32)
    block_start = jnp.arange(nb, dtype=I32) * TM_MOE
    block_e = jnp.minimum(jnp.sum((pends[None, :] <= block_start[:, None]).astype(I32), axis=1),
                          N_EXPERTS - 1)

    dest_tiles = _dest(pstarts, ids, rank)
    xs = _dispatch(counts, pstarts, padded, nb_used, dest_tiles, xn2, meta, nb * TM_MOE)
    planes = _experts(block_e, nb_used, xs, moe_w_gu[0],
                      moe_b_gu[0].reshape(N_EXPERTS, 1, 2 * D_FF),
                      moe_w_down[0], moe_b_down[0].reshape(N_EXPERTS, 1, D), T)
    out = _combine(h1, planes, final_norm_w.reshape(1, D))
    return out.reshape(B, S, D)
```
